```python
import math
import functools
import jax
import jax.numpy as jnp
from jax import lax
import numpy as np

D_MODEL = 2048
BATCH = 2
SEQ = 4096
DEPTH = 1
DEC_BATCH = 32
DEC_SEQ = 1
PAST_LEN = 16384
PAGE_SIZE = 128

N_HEADS_A = 8
HEAD_DIM_A = 64
A_QK = N_HEADS_A * 2 * HEAD_DIM_A
A_V = N_HEADS_A * 2 * HEAD_DIM_A
Q_BLOCK = 128
N_HEADS_B = 8
DK_B = 128
DV_B = 128
B_K = N_HEADS_B * DK_B
B_V = N_HEADS_B * DV_B
CONV_W = 4
CONV_DIM = 2 * B_K + B_V
CHUNK = 64
IN_SPLITS = (A_QK, A_QK, A_V, A_V, CONV_DIM, B_V, N_HEADS_B, N_HEADS_B, D_MODEL, D_MODEL)
IN_DIM = sum(IN_SPLITS)
EPS = 1e-6

kernel_name = 'hybrid_diffattn_gated_delta_adaln_step'


def rms_norm(x, w):
    xf = x.astype(jnp.float32)
    y = xf * lax.rsqrt(jnp.mean(xf * xf, axis=-1, keepdims=True) + EPS)
    return (y * w.astype(jnp.float32)).astype(x.dtype)


def l2_norm(x):
    xf = x.astype(jnp.float32)
    return (xf * lax.rsqrt(jnp.sum(xf * xf, axis=-1, keepdims=True) + EPS)).astype(x.dtype)


def causal_conv(u, buf, w):
    t = u.shape[1]
    up = jnp.concatenate([buf.astype(u.dtype), u], axis=1)
    out = sum(up[:, j:j + t] * w[j] for j in range(CONV_W))
    return jax.nn.silu(out), up[:, -(CONV_W - 1):]


def gated_delta_chunked(q, k, v, g, beta, s0):
    f32 = jnp.float32
    bsz, t, nh, dk = q.shape
    dv = v.shape[-1]
    n = -(-t // CHUNK)
    pad = n * CHUNK - t
    def prep(a):
        a = jnp.pad(a.astype(f32), [(0, 0), (0, pad)] + [(0, 0)] * (a.ndim - 2))
        a = a.reshape((bsz, n, CHUNK) + a.shape[2:])
        return jnp.moveaxis(jnp.moveaxis(a, 1, 0), 3, 2)
    q, k, v, g, beta = prep(q * dk ** -0.5), prep(k), prep(v), prep(g), prep(beta)
    kb = k * beta[..., None]
    vb = v * beta[..., None]
    gc = jnp.cumsum(g, axis=-1)
    tril = jnp.tril(jnp.ones((CHUNK, CHUNK), bool))
    strict = jnp.tril(jnp.ones((CHUNK, CHUNK), bool), -1)
    decay = jnp.exp(jnp.where(tril, gc[..., :, None] - gc[..., None, :], -jnp.inf))
    lmat = jnp.where(strict, jnp.einsum('nbhid,nbhjd->nbhij', kb, k) * decay, 0.0)
    eye = jnp.eye(CHUNK, dtype=f32)
    tinv = lax.linalg.triangular_solve(eye + lmat, jnp.broadcast_to(eye, lmat.shape),
                                       left_side=True, lower=True, unit_diagonal=True)
    u = jnp.matmul(tinv, vb)
    wk = jnp.matmul(tinv, kb * jnp.exp(gc)[..., None])
    qk = jnp.where(tril, jnp.einsum('nbhid,nbhjd->nbhij', q, k) * decay, 0.0)

    def step(s, xs):
        q_c, k_c, u_c, w_c, gc_c, qk_c = xs
        v_new = u_c - jnp.matmul(w_c, s)
        o = jnp.matmul(q_c * jnp.exp(gc_c)[..., None], s) + jnp.matmul(qk_c, v_new)
        g_last = gc_c[..., -1]
        k_dec = k_c * jnp.exp(g_last[..., None] - gc_c)[..., None]
        s = s * jnp.exp(g_last)[..., None, None] + jnp.einsum('bhcd,bhce->bhde', k_dec, v_new)
        return s, o

    s_fin, o = lax.scan(step, s0.astype(f32), (q, k, u, wk, gc, qk))
    o = jnp.moveaxis(jnp.moveaxis(o, 3, 2), 0, 1).reshape(bsz, n * CHUNK, nh, dv)[:, :t]
    return o, s_fin


def diff_attn_prompt(q, k, v, lam):
    bsz, s_len, nh = q.shape[:3]
    nb = s_len // Q_BLOCK
    scale = HEAD_DIM_A ** -0.5
    qb = jnp.moveaxis(q.reshape((bsz, nb, Q_BLOCK) + q.shape[2:]), 1, 0)
    kpos = jnp.arange(s_len)

    def block(args):
        q_blk, start = args
        s = jnp.einsum('bqhcd,bkhcd->bhcqk', q_blk, k).astype(jnp.float32) * scale
        qpos = start + jnp.arange(Q_BLOCK)
        s = jnp.where(kpos[None, :] <= qpos[:, None], s, -jnp.inf)
        p = jax.nn.softmax(s, axis=-1)
        pd = p[:, :, 0] - lam * p[:, :, 1]
        return jnp.einsum('bhqk,bkhe->bqhe', pd.astype(v.dtype), v)

    out = lax.map(block, (qb, jnp.arange(nb) * Q_BLOCK))
    return jnp.moveaxis(out, 0, 1).reshape(bsz, s_len, nh, v.shape[-1])


def diff_attn_sample(q, k, v, lam, k_past, v_past):
    n_past = k_past.shape[1]
    t = q.shape[1]
    scale = HEAD_DIM_A ** -0.5
    s_past = jnp.einsum('bqhcd,bkhcd->bhcqk', q, k_past)
    s_new = jnp.einsum('bqhcd,bkhcd->bhcqk', q, k)
    s_new = jnp.where(jnp.tril(jnp.ones((t, t), bool)), s_new, -jnp.inf)
    s = jnp.concatenate([s_past, s_new], axis=-1).astype(jnp.float32) * scale
    p = jax.nn.softmax(s, axis=-1)
    pd = (p[:, :, 0] - lam * p[:, :, 1]).astype(v.dtype)
    return (jnp.einsum('bhqk,bkhe->bqhe', pd[..., :n_past], v_past)
            + jnp.einsum('bhqk,bkhe->bqhe', pd[..., n_past:], v))


def mixer_sublayer(x, c, attend, conv_buf, s0, lambda_init,
                   w_ada, b_ada, norm_w, w_in, q_norm_w, k_norm_w,
                   lam_q1, lam_k1, lam_q2, lam_k2, subln_w, conv_w,
                   a_log, dt_bias, o_norm_w, w_branch_a, w_branch_b, w_out):
    bsz, t, _ = x.shape
    f32 = jnp.float32
    mod = jax.nn.silu(c) @ w_ada + b_ada
    shift, scale, gate = jnp.split(mod[:, None, :], 3, axis=-1)
    h = rms_norm(x, norm_w) * (1 + scale) + shift
    proj = h @ w_in
    qa, ka, va, za, qkv_b, zb, a_b, b_b, ga, gb = jnp.split(
        proj, np.cumsum(IN_SPLITS)[:-1].tolist(), axis=-1)

    qa = rms_norm(qa.reshape(bsz, t, N_HEADS_A, 2, HEAD_DIM_A), q_norm_w)
    ka = rms_norm(ka.reshape(bsz, t, N_HEADS_A, 2, HEAD_DIM_A), k_norm_w)
    va = va.reshape(bsz, t, N_HEADS_A, 2 * HEAD_DIM_A)
    lam = (jnp.exp(jnp.sum(lam_q1.astype(f32) * lam_k1.astype(f32)))
           - jnp.exp(jnp.sum(lam_q2.astype(f32) * lam_k2.astype(f32))) + lambda_init)
    oa = attend(qa, ka, va, lam)
    oa = rms_norm(oa, subln_w) * (1.0 - lambda_init)
    ya = (oa.reshape(bsz, t, A_V) * jax.nn.silu(za)) @ w_branch_a

    qkv, conv_state = causal_conv(qkv_b, conv_buf, conv_w)
    qb, kb, vb = jnp.split(qkv, [B_K, 2 * B_K], axis=-1)
    qb = l2_norm(qb.reshape(bsz, t, N_HEADS_B, DK_B))
    kb = l2_norm(kb.reshape(bsz, t, N_HEADS_B, DK_B))
    vb = vb.reshape(bsz, t, N_HEADS_B, DV_B)
    g = -jnp.exp(a_log.astype(f32)) * jax.nn.softplus(a_b.astype(f32) + dt_bias.astype(f32))
    beta = jax.nn.sigmoid(b_b.astype(f32))
    ob, s_new = gated_delta_chunked(qb, kb, vb, g, beta, s0)
    ob = rms_norm(ob.astype(x.dtype), o_norm_w) * jax.nn.silu(zb.reshape(bsz, t, N_HEADS_B, DV_B))
    yb = ob.reshape(bsz, t, B_V) @ w_branch_b

    m = jax.nn.sigmoid(ga) * ya + jax.nn.sigmoid(gb) * yb
    y = x + gate * (m @ w_out)
    return y, ka, va, s_new, conv_state


def setup_inputs(seed: int = 0) -> dict:
    key = jax.random.key(seed)
    ks = jax.random.split(key, 32)
    f32 = jnp.float32
    n_pages = PAST_LEN // PAGE_SIZE
    n_used = DEC_BATCH * n_pages
    n_pool = n_used + n_used // 4
    L = DEPTH

    def nrm(k, shape, s=1.0):
        return s * jax.random.normal(k, shape, f32)

    dt = jnp.exp(jax.random.uniform(ks[22], (L, N_HEADS_B), f32, math.log(1e-3), math.log(1e-1)))
    return {
        'x_prompt': nrm(ks[0], (BATCH, SEQ, D_MODEL)),
        'x_sample': nrm(ks[1], (DEC_BATCH, DEC_SEQ, D_MODEL)),
        'c_prompt': nrm(ks[2], (BATCH, D_MODEL)),
        'c_sample': nrm(ks[3], (DEC_BATCH, D_MODEL)),
        'cache_k': nrm(ks[4], (L, n_pool, PAGE_SIZE, N_HEADS_A, 2, HEAD_DIM_A)),
        'cache_v': nrm(ks[5], (L, n_pool, PAGE_SIZE, N_HEADS_A, 2 * HEAD_DIM_A)),
        'page_table': jax.random.permutation(ks[6], n_pool)[:n_used].reshape(DEC_BATCH, n_pages).astype(jnp.int32),
        'state_ssm': nrm(ks[7], (L, DEC_BATCH, N_HEADS_B, DK_B, DV_B), 0.1),
        'state_conv': nrm(ks[8], (L, DEC_BATCH, CONV_W - 1, CONV_DIM)),
        'w_ada': nrm(ks[9], (L, D_MODEL, 3 * D_MODEL), 0.5 * D_MODEL ** -0.5),
        'b_ada': nrm(ks[10], (L, 3 * D_MODEL), 0.02),
        'norm_w': 1.0 + nrm(ks[11], (L, D_MODEL), 0.02),
        'w_in': nrm(ks[12], (L, D_MODEL, IN_DIM), D_MODEL ** -0.5),
        'q_norm_w': 1.0 + nrm(ks[13], (L, HEAD_DIM_A), 0.02),
        'k_norm_w': 1.0 + nrm(ks[14], (L, HEAD_DIM_A), 0.02),
        'lam_q1': nrm(ks[15], (L, HEAD_DIM_A), 0.1),
        'lam_k1': nrm(ks[16], (L, HEAD_DIM_A), 0.1),
        'lam_q2': nrm(ks[17], (L, HEAD_DIM_A), 0.1),
        'lam_k2': nrm(ks[18], (L, HEAD_DIM_A), 0.1),
        'subln_w': 1.0 + nrm(ks[19], (L, 2 * HEAD_DIM_A), 0.02),
        'conv_w': nrm(ks[20], (L, CONV_W, CONV_DIM), CONV_W ** -0.5),
        'a_log': jnp.log(jax.random.uniform(ks[21], (L, N_HEADS_B), f32, 1.0, 16.0)),
        'dt_bias': dt + jnp.log(-jnp.expm1(-dt)),
        'o_norm_w': 1.0 + nrm(ks[23], (L, DV_B), 0.02),
        'w_branch_a': nrm(ks[24], (L, A_V, D_MODEL), A_V ** -0.5),
        'w_branch_b': nrm(ks[25], (L, B_V, D_MODEL), B_V ** -0.5),
        'w_out': nrm(ks[26], (L, D_MODEL, D_MODEL), D_MODEL ** -0.5),
    }


def reference(x_prompt, x_sample, c_prompt, c_sample, cache_k, cache_v, page_table,
              state_ssm, state_conv, w_ada, b_ada, norm_w, w_in, q_norm_w, k_norm_w,
              lam_q1, lam_k1, lam_q2, lam_k2, subln_w, conv_w, a_log, dt_bias,
              o_norm_w, w_branch_a, w_branch_b, w_out):
    bsz = x_prompt.shape[0]
    dec_b = x_sample.shape[0]
    past = page_table.shape[1] * PAGE_SIZE
    weights = (w_ada, b_ada, norm_w, w_in, q_norm_w, k_norm_w, lam_q1, lam_k1, lam_q2,
               lam_k2, subln_w, conv_w, a_log, dt_bias, o_norm_w, w_branch_a, w_branch_b, w_out)
    yp, ys = x_prompt, x_sample
    kp_l, vp_l, ks_l, vs_l, sp_l, ss_l, cp_l, cs_l = [], [], [], [], [], [], [], []
    for li in range(DEPTH):
        lw = [w[li] for w in weights]
        lambda_init = 0.8 - 0.6 * math.exp(-0.3 * li)
        s0 = jnp.zeros((bsz, N_HEADS_B, DK_B, DV_B), jnp.float32)
        buf0 = jnp.zeros((bsz, CONV_W - 1, CONV_DIM), x_prompt.dtype)
        yp, kp, vp, sp, cp = mixer_sublayer(yp, c_prompt, diff_attn_prompt, buf0, s0, lambda_init, *lw)
        k_past = cache_k[li][page_table].reshape(dec_b, past, N_HEADS_A, 2, HEAD_DIM_A)
        v_past = cache_v[li][page_table].reshape(dec_b, past, N_HEADS_A, 2 * HEAD_DIM_A)
        attend_s = functools.partial(diff_attn_sample, k_past=k_past, v_past=v_past)
        ys, ks_, vs_, ss_, cs_ = mixer_sublayer(ys, c_sample, attend_s, state_conv[li], state_ssm[li],
                                                lambda_init, *lw)
        kp_l.append(kp); vp_l.append(vp); ks_l.append(ks_); vs_l.append(vs_)
        sp_l.append(sp); ss_l.append(ss_); cp_l.append(cp); cs_l.append(cs_)
    return (yp, ys, jnp.stack(kp_l), jnp.stack(vp_l), jnp.stack(ks_l), jnp.stack(vs_l),
            jnp.stack(sp_l), jnp.stack(ss_l), jnp.stack(cp_l), jnp.stack(cs_l))
```

```python
import functools
import math

import jax
import jax.numpy as jnp
from jax import lax
from jax.experimental import pallas as pl
from jax.experimental.pallas import tpu as pltpu

F32 = jnp.float32
BF16 = jnp.bfloat16

N_HEADS_A = 8
HEAD_DIM_A = 64
A_W = N_HEADS_A * 2 * HEAD_DIM_A
N_HEADS_B = 8
DK_B = 128
DV_B = 128
B_W = N_HEADS_B * DK_B
CONV_W = 4
CHUNK = 64
PAGE_SIZE = 128
EPS = 1e-6
LANES = 128
SUBLANES = 8
VMEM_LIMIT = 56 * 1024 * 1024
PAGES_PER_STEP = 4

_NT = (((1,), (1,)), ((), ()))
_TN = (((0,), (0,)), ((), ()))


def _cparams(sem):
    return pltpu.CompilerParams(dimension_semantics=sem, vmem_limit_bytes=VMEM_LIMIT)


def _dot(a, b):
    return jnp.dot(a, b, preferred_element_type=F32)


def _dot_nt(a, b):
    return lax.dot_general(a, b, _NT, preferred_element_type=F32)


def _dot_tn(a, b):
    return lax.dot_general(a, b, _TN, preferred_element_type=F32)


def _sigmoid(x):
    return 1.0 / (1.0 + jnp.exp(-x))


def _silu(x):
    return x * _sigmoid(x)


def _ada_kernel(c_ref, w_ref, b_ref, o_ref):
    sc = _silu(c_ref[...]).astype(BF16)
    o_ref[...] = _dot(sc, w_ref[...].astype(BF16)) + b_ref[...]


def _ada(c, w_ada, b_ada):
    m, d = c.shape
    n = w_ada.shape[1]
    tn = 512
    return pl.pallas_call(
        _ada_kernel,
        grid=(n // tn,),
        in_specs=[pl.BlockSpec((m, d), lambda j: (0, 0)),
                  pl.BlockSpec((d, tn), lambda j: (0, j)),
                  pl.BlockSpec((1, tn), lambda j: (0, j))],
        out_specs=pl.BlockSpec((m, tn), lambda j: (0, j)),
        out_shape=jax.ShapeDtypeStruct((m, n), F32),
        compiler_params=_cparams(("parallel",)),
        name="ada",
    )(c, w_ada, b_ada.reshape(1, n))


def _hnorm_kernel(x_ref, shift_ref, scale_ref, nw_ref, h_ref):
    x = x_ref[0]
    ms = jnp.mean(x * x, axis=-1, keepdims=True)
    y = x * lax.rsqrt(ms + EPS) * nw_ref[...]
    h_ref[0] = (y * (1.0 + scale_ref[0]) + shift_ref[0]).astype(BF16)


def _hnorm(x, shift, scale, norm_w, tm):
    b, s, d = x.shape
    ts = tm if shift.shape[1] == s else 1
    mod_map = (lambda i, j: (i, j, 0)) if ts == tm else (lambda i, j: (i, 0, 0))
    return pl.pallas_call(
        _hnorm_kernel,
        grid=(b, s // tm),
        in_specs=[pl.BlockSpec((1, tm, d), lambda i, j: (i, j, 0)),
                  pl.BlockSpec((1, ts, d), mod_map),
                  pl.BlockSpec((1, ts, d), mod_map),
                  pl.BlockSpec((1, d), lambda i, j: (0, 0))],
        out_specs=pl.BlockSpec((1, tm, d), lambda i, j: (i, j, 0)),
        out_shape=jax.ShapeDtypeStruct((b, s, d), BF16),
        compiler_params=_cparams(("parallel", "parallel")),
        name="hnorm",
    )(x, shift, scale, norm_w.reshape(1, d))


def _group_rms(acc, g_ref):
    sq = acc * acc
    hi = sq.astype(BF16)
    lo = (sq - hi.astype(F32)).astype(BF16)
    g = g_ref[...]
    outs = []
    for hh in range(acc.shape[1] // LANES):
        sl = slice(hh * LANES, (hh + 1) * LANES)
        ms = _dot(hi[:, sl], g) + _dot(lo[:, sl], g)
        outs.append(acc[:, sl] * lax.rsqrt(ms + EPS))
    return outs


def _proj_kernel(kind, h_ref, w_ref, *refs):
    acc = _dot(h_ref[...], w_ref[...])
    if kind == "q":
        g_ref, nw_ref, o_ref = refs
        for hh, y in enumerate(_group_rms(acc, g_ref)):
            sl = slice(hh * LANES, (hh + 1) * LANES)
            o_ref[:, sl] = (y * nw_ref[:, sl] * (HEAD_DIM_A ** -0.5)).astype(BF16)
    elif kind == "k":
        g_ref, nw_ref, o_ref, ob_ref = refs
        for hh, y in enumerate(_group_rms(acc, g_ref)):
            sl = slice(hh * LANES, (hh + 1) * LANES)
            y = y * nw_ref[:, sl]
            o_ref[:, sl] = y
            ob_ref[:, sl] = y.astype(BF16)
    elif kind == "v":
        o_ref, ob_ref = refs
        o_ref[...] = acc
        ob_ref[...] = acc.astype(BF16)
    elif kind == "silu":
        (o_ref,) = refs
        o_ref[...] = _silu(acc)
    elif kind == "sigmoid":
        (o_ref,) = refs
        o_ref[...] = _sigmoid(acc)
    elif kind == "raw":
        (o_ref,) = refs
        o_ref[...] = acc
    elif kind == "ab":
        alog_ref, dtb_ref, o_ref = refs
        a = acc + dtb_ref[...]
        sp = jnp.maximum(a, 0.0) + jnp.log(1.0 + jnp.exp(-jnp.abs(a)))
        g = -jnp.exp(alog_ref[...]) * sp
        lane = lax.broadcasted_iota(jnp.int32, acc.shape, 1)
        o_ref[...] = jnp.where(lane < N_HEADS_B, g, _sigmoid(acc))
    else:
        raise ValueError(kind)


def _proj_kt_kernel(h_ref, wt_ref, nw_ref, o_ref, ob_ref):
    acc = _dot_nt(wt_ref[...], h_ref[...])
    tm = acc.shape[1]
    x3 = acc.reshape(2 * N_HEADS_A, HEAD_DIM_A, tm)
    ms = jnp.mean(x3 * x3, axis=1, keepdims=True)
    y = (x3 * lax.rsqrt(ms + EPS) * nw_ref[...]).reshape(A_W, tm)
    o_ref[0] = y
    ob_ref[0] = y.astype(BF16)


def _proj_kt(h, wt, nw_col, bsz, s_len, tm):
    k = h.shape[1]
    nt = s_len // tm
    out_spec = pl.BlockSpec((1, A_W, tm), lambda i: (i // nt, 0, i % nt))
    return pl.pallas_call(
        _proj_kt_kernel,
        grid=(bsz * nt,),
        in_specs=[pl.BlockSpec((tm, k), lambda i: (i, 0)),
                  pl.BlockSpec((A_W, k), lambda i: (0, 0)),
                  pl.BlockSpec(nw_col.shape, lambda i: (0, 0, 0))],
        out_specs=[out_spec, out_spec],
        out_shape=[jax.ShapeDtypeStruct((bsz, A_W, s_len), F32),
                   jax.ShapeDtypeStruct((bsz, A_W, s_len), BF16)],
        compiler_params=_cparams(("parallel",)),
        name="proj_kt",
    )(h, wt, nw_col)


def _proj(kind, h, w, extras, out_dtypes, tm, tn):
    m, k = h.shape
    n = w.shape[1]
    extra_specs = [pl.BlockSpec(e.shape, lambda j, i, nd=e.ndim: (0,) * nd) for e in extras]
    if kind in ("q", "k"):
        extra_specs[1] = pl.BlockSpec((1, tn), lambda j, i: (0, j))
    outs = pl.pallas_call(
        functools.partial(_proj_kernel, kind),
        grid=(n // tn, m // tm),
        in_specs=[pl.BlockSpec((tm, k), lambda j, i: (i, 0)),
                  pl.BlockSpec((k, tn), lambda j, i: (0, j))] + extra_specs,
        out_specs=[pl.BlockSpec((tm, tn), lambda j, i: (i, j)) for _ in out_dtypes],
        out_shape=[jax.ShapeDtypeStruct((m, n), dt) for dt in out_dtypes],
        compiler_params=_cparams(("parallel", "parallel")),
        name="proj_" + kind,
    )(h, w, *extras)
    return outs


def _lam(lamp_ref, lambda_init):
    lp = lamp_ref[...]
    return (jnp.exp(jnp.sum(lp[0:1] * lp[1:2], axis=-1, keepdims=True))
            - jnp.exp(jnp.sum(lp[2:3] * lp[3:4], axis=-1, keepdims=True)) + lambda_init)


def _subln(o, sw, lambda_init):
    ms = jnp.mean(o * o, axis=-1, keepdims=True)
    return o * lax.rsqrt(ms + EPS) * sw * (1.0 - lambda_init)


def _attn_kernel(lamp_ref, q_ref, k_ref, v_ref, sw_ref, o_ref, m_sc, l_sc, acc_sc, *, tq, lambda_init):
    i = pl.program_id(2)
    q = q_ref[...]
    lane = lax.broadcasted_iota(jnp.int32, q.shape, 1)
    zero = jnp.zeros_like(q)
    qs = (jnp.where(lane < HEAD_DIM_A, q, zero), jnp.where(lane >= HEAD_DIM_A, q, zero))
    m_sc[...] = jnp.full(m_sc.shape, -jnp.inf, F32)
    l_sc[...] = jnp.zeros(l_sc.shape, F32)
    acc_sc[...] = jnp.zeros(acc_sc.shape, F32)

    def step(j, masked):
        start = pl.multiple_of(j * tq, tq)
        kb = k_ref[0, :, pl.ds(start, tq)]
        vb = v_ref[pl.ds(start, tq), :]
        for c in range(2):
            s = _dot(qs[c], kb)
            if masked:
                row = lax.broadcasted_iota(jnp.int32, s.shape, 0)
                col = lax.broadcasted_iota(jnp.int32, s.shape, 1)
                s = jnp.where(col <= row, s, -jnp.inf)
            m_prev = m_sc[c]
            m_new = jnp.maximum(m_prev, jnp.max(s, axis=-1, keepdims=True))
            alpha = jnp.exp(m_prev - m_new)
            p = jnp.exp(s - m_new)
            l_sc[c] = alpha * l_sc[c] + jnp.sum(p, axis=-1, keepdims=True)
            acc_sc[c] = alpha * acc_sc[c] + _dot(p.astype(BF16), vb)
            m_sc[c] = m_new

    def body(j, carry):
        step(j, False)
        return carry

    lax.fori_loop(0, i, body, 0)
    step(i, True)
    lam = _lam(lamp_ref, lambda_init)
    o = acc_sc[0] / l_sc[0] - lam * (acc_sc[1] / l_sc[1])
    o_ref[...] = _subln(o, sw_ref[...], lambda_init)


def _attn_prompt(q, k, v, lamp, subln_w, bsz, s_len, tq, lambda_init):
    m = q.shape[0]
    nq = s_len // tq
    return pl.pallas_call(
        functools.partial(_attn_kernel, tq=tq, lambda_init=lambda_init),
        grid=(bsz, N_HEADS_A, nq),
        in_specs=[pl.BlockSpec(lamp.shape, lambda b, h, i: (0, 0)),
                  pl.BlockSpec((tq, LANES), lambda b, h, i: (b * nq + i, h)),
                  pl.BlockSpec((1, LANES, s_len), lambda b, h, i: (b, h, 0)),
                  pl.BlockSpec((s_len, LANES), lambda b, h, i: (b, h)),
                  pl.BlockSpec((1, LANES), lambda b, h, i: (0, 0))],
        out_specs=pl.BlockSpec((tq, LANES), lambda b, h, i: (b * nq + i, h)),
        out_shape=jax.ShapeDtypeStruct((m, A_W), F32),
        scratch_shapes=[pltpu.VMEM((2, tq, 1), F32), pltpu.VMEM((2, tq, 1), F32),
                        pltpu.VMEM((2, tq, LANES), F32)],
        compiler_params=_cparams(("parallel", "parallel", "parallel")),
        name="attn_prompt",
    )(lamp, q, k, v, subln_w.reshape(1, LANES))


def _dec_attn_kernel(pt_ref, lamp_ref, q_ref, kn_ref, vn_ref, sw_ref, *refs, n_pages_step, lambda_init):
    k_refs = refs[:n_pages_step]
    v_refs = refs[n_pages_step:2 * n_pages_step]
    o_ref, m_sc, l_sc, acc_sc = refs[2 * n_pages_step:]
    j = pl.program_id(1)
    nrow = 2 * N_HEADS_A
    q = q_ref[0]
    hc = lax.broadcasted_iota(jnp.int32, (nrow, A_W), 0)
    lane = lax.broadcasted_iota(jnp.int32, (nrow, A_W), 1)
    qbd = jnp.where(lane // HEAD_DIM_A == hc, jnp.broadcast_to(q, (nrow, A_W)), 0.0).astype(BF16)

    def update(s, vt):
        m_prev = m_sc[...]
        m_new = jnp.maximum(m_prev, jnp.max(s, axis=-1, keepdims=True))
        alpha = jnp.exp(m_prev - m_new)
        p = jnp.exp(s - m_new)
        l_sc[...] = alpha * l_sc[...] + jnp.sum(p, axis=-1, keepdims=True)
        acc_sc[...] = alpha * acc_sc[...] + _dot(p.astype(BF16), vt)
        m_sc[...] = m_new

    @pl.when(j == 0)
    def _():
        m_sc[...] = jnp.full(m_sc.shape, -jnp.inf, F32)
        l_sc[...] = jnp.zeros(l_sc.shape, F32)
        acc_sc[...] = jnp.zeros(acc_sc.shape, F32)
        kn = jnp.broadcast_to(kn_ref[0], (SUBLANES, A_W)).astype(BF16)
        vn = jnp.broadcast_to(vn_ref[0], (SUBLANES, A_W)).astype(BF16)
        col = lax.broadcasted_iota(jnp.int32, (nrow, SUBLANES), 1)
        update(jnp.where(col == 0, _dot_nt(qbd, kn), -jnp.inf), vn)

    for p_i in range(n_pages_step):
        update(_dot(qbd, k_refs[p_i][0].astype(BF16)), v_refs[p_i][0].astype(BF16))

    @pl.when(j == pl.num_programs(1) - 1)
    def _():
        lam = _lam(lamp_ref, lambda_init)
        o = acc_sc[...] / l_sc[...]
        for h in range(N_HEADS_A):
            sl = slice(h * LANES, (h + 1) * LANES)
            oh = o[2 * h:2 * h + 1, sl] - lam * o[2 * h + 1:2 * h + 2, sl]
            o_ref[0, :, sl] = _subln(oh, sw_ref[...], lambda_init)


def _attn_sample(q, kn, vn, cache_k, cache_v, page_table, lamp, subln_w, lambda_init):
    nb, n_pages = page_table.shape
    pps = PAGES_PER_STEP if n_pages % PAGES_PER_STEP == 0 else 1
    row_spec = pl.BlockSpec((1, 1, A_W), lambda b, j, pt: (b, 0, 0))

    def page_spec(p_i, shape):
        return pl.BlockSpec((1,) + shape, lambda b, j, pt: (pt[b, j * pps + p_i], 0, 0))

    grid_spec = pltpu.PrefetchScalarGridSpec(
        num_scalar_prefetch=1,
        grid=(nb, n_pages // pps),
        in_specs=[pl.BlockSpec(lamp.shape, lambda b, j, pt: (0, 0)), row_spec, row_spec, row_spec,
                  pl.BlockSpec((1, LANES), lambda b, j, pt: (0, 0))]
                 + [page_spec(p_i, (A_W, PAGE_SIZE)) for p_i in range(pps)]
                 + [page_spec(p_i, (PAGE_SIZE, A_W)) for p_i in range(pps)],
        out_specs=row_spec,
        scratch_shapes=[pltpu.VMEM((2 * N_HEADS_A, 1), F32), pltpu.VMEM((2 * N_HEADS_A, 1), F32),
                        pltpu.VMEM((2 * N_HEADS_A, A_W), F32)],
    )
    return pl.pallas_call(
        functools.partial(_dec_attn_kernel, n_pages_step=pps, lambda_init=lambda_init),
        grid_spec=grid_spec,
        out_shape=jax.ShapeDtypeStruct((nb, 1, A_W), F32),
        compiler_params=_cparams(("parallel", "arbitrary")),
        name="attn_sample",
    )(page_table, lamp, q, kn, vn, subln_w.reshape(1, LANES),
      *([cache_k] * pps), *([cache_v] * pps))


def _l2n(y):
    return y * lax.rsqrt(jnp.sum(y * y, axis=-1, keepdims=True) + EPS)


def _conv_kernel(x_ref, w_ref, o_ref, ext_ref, *, tm):
    seg = pl.program_id(1)
    t = pl.program_id(2)

    @pl.when(t == 0)
    def _():
        ext_ref[0:SUBLANES, :] = jnp.zeros((SUBLANES, ext_ref.shape[1]), F32)

    ext_ref[SUBLANES:SUBLANES + tm, :] = x_ref[...]
    w = w_ref[...]
    y = ext_ref[SUBLANES - 3:SUBLANES - 3 + tm, :] * w[0:1]
    for jj in range(1, CONV_W):
        off = SUBLANES - 3 + jj
        y = y + ext_ref[off:off + tm, :] * w[jj:jj + 1]
    y = _silu(y)
    qscale = jnp.where(seg == 0, DK_B ** -0.5, 1.0)
    for h in range(N_HEADS_B):
        sl = slice(h * LANES, (h + 1) * LANES)
        yh = y[:, sl]
        o_ref[:, sl] = jnp.where(seg < 2, _l2n(yh) * qscale, yh)
    ext_ref[0:SUBLANES, :] = ext_ref[tm:tm + SUBLANES, :]


def _conv_prompt(raw, conv_w, bsz, s_len, tm):
    m, cdim = raw.shape
    nt = s_len // tm
    return pl.pallas_call(
        functools.partial(_conv_kernel, tm=tm),
        grid=(bsz, cdim // B_W, nt),
        in_specs=[pl.BlockSpec((tm, B_W), lambda b, g, t: (b * nt + t, g)),
                  pl.BlockSpec((CONV_W, B_W), lambda b, g, t: (0, g))],
        out_specs=pl.BlockSpec((tm, B_W), lambda b, g, t: (b * nt + t, g)),
        out_shape=jax.ShapeDtypeStruct((m, cdim), F32),
        scratch_shapes=[pltpu.VMEM((tm + SUBLANES, B_W), F32)],
        compiler_params=_cparams(("parallel", "parallel", "arbitrary")),
        name="conv_prompt",
    )(raw, conv_w)


def _split_dot(a, b):
    ah = a.astype(BF16)
    al = (a - ah.astype(F32)).astype(BF16)
    bh = b.astype(BF16)
    bl = (b - bh.astype(F32)).astype(BF16)
    return _dot(ah, bh) + _dot(ah, bl) + _dot(al, bh)


def _cumsum_rows(x):
    n = x.shape[0]
    row = lax.broadcasted_iota(jnp.int32, x.shape, 0)
    s = 1
    while s < n:
        x = x + jnp.where(row >= s, pltpu.roll(x, s, axis=0), 0.0)
        s *= 2
    return x


def _delta_kernel(q_ref, k_ref, v_ref, gb_ref, zb_ref, onw_ref, ob_ref, s_ref):
    c = pl.program_id(1)

    @pl.when(c == 0)
    def _():
        s_ref[...] = jnp.zeros(s_ref.shape, F32)

    gb = gb_ref[...]
    row = lax.broadcasted_iota(jnp.int32, (CHUNK, CHUNK), 0)
    col = lax.broadcasted_iota(jnp.int32, (CHUNK, CHUNK), 1)
    tril = col <= row
    strict = col < row
    eye = jnp.where(row == col, 1.0, 0.0)
    for h in range(N_HEADS_B):
        sl = slice(h * LANES, (h + 1) * LANES)
        q = q_ref[:, sl]
        k = k_ref[:, sl]
        v = v_ref[:, sl]
        beta = gb[:, N_HEADS_B + h:N_HEADS_B + h + 1]
        gc = _cumsum_rows(jnp.broadcast_to(gb[:, h:h + 1], (CHUNK, LANES)))
        gc_row = gc.T[0:1, :]
        decay = jnp.exp(jnp.where(tril, gc[:, :CHUNK] - gc_row, -jnp.inf))
        kb = k * beta
        k16 = k.astype(BF16)
        lmat = jnp.where(strict, _dot_nt(kb.astype(BF16), k16) * decay, 0.0)
        tinv = eye - lmat
        pw = lmat
        for _ in range(5):
            pw = _split_dot(pw, pw)
            tinv = tinv + _split_dot(tinv, pw)
        t16 = tinv.astype(BF16)
        u = _dot(t16, (v * beta).astype(BF16))
        wk = _dot(t16, (kb * jnp.exp(gc)).astype(BF16))
        qk = jnp.where(tril, _dot_nt(q.astype(BF16), k16) * decay, 0.0)
        s_old = s_ref[0, h]
        s16 = s_old.astype(BF16)
        v_new = u - _dot(wk.astype(BF16), s16)
        vn16 = v_new.astype(BF16)
        o = _dot((q * jnp.exp(gc)).astype(BF16), s16) + _dot(qk.astype(BF16), vn16)
        g_last = gc[CHUNK - 1:CHUNK, :]
        k_dec = k * jnp.exp(g_last - gc)
        s_ref[0, h] = s_old * jnp.exp(g_last) + _dot_tn(k_dec.astype(BF16), vn16)
        on = o * lax.rsqrt(jnp.mean(o * o, axis=-1, keepdims=True) + EPS) * onw_ref[...]
        ob_ref[:, sl] = (on * zb_ref[:, sl]).astype(BF16)


def _delta_prompt(qkv, gb, zb, o_norm_w, bsz, s_len):
    m = qkv.shape[0]
    nc = s_len // CHUNK
    seg = lambda g: pl.BlockSpec((CHUNK, B_W), lambda b, c, g=g: (b * nc + c, g))
    return pl.pallas_call(
        _delta_kernel,
        grid=(bsz, nc),
        in_specs=[seg(0), seg(1), seg(2),
                  pl.BlockSpec((CHUNK, LANES), lambda b, c: (b * nc + c, 0)),
                  seg(0),
                  pl.BlockSpec((1, LANES), lambda b, c: (0, 0))],
        out_specs=[seg(0), pl.BlockSpec((1, N_HEADS_B, DK_B, DV_B), lambda b, c: (b, 0, 0, 0))],
        out_shape=[jax.ShapeDtypeStruct((m, B_W), BF16),
                   jax.ShapeDtypeStruct((bsz, N_HEADS_B, DK_B, DV_B), F32)],
        compiler_params=_cparams(("parallel", "arbitrary")),
        name="delta_prompt",
    )(qkv, qkv, qkv, gb, zb, o_norm_w.reshape(1, LANES))


def _dec_delta_kernel(u_ref, buf_ref, w_ref, gb_ref, zb_ref, onw_ref, s0_ref, ob_ref, s_ref):
    w = w_ref[...]
    buf = buf_ref[0]
    y = buf[0:1] * w[0:1] + buf[1:2] * w[1:2] + buf[2:3] * w[2:3] + u_ref[0] * w[3:4]
    y = _silu(y)
    gb = gb_ref[0]
    r = lax.broadcasted_iota(jnp.int32, (DK_B, LANES), 0)
    cidx = lax.broadcasted_iota(jnp.int32, (DK_B, LANES), 1)
    eye = r == cidx

    def column(x):
        return jnp.sum(jnp.where(eye, jnp.broadcast_to(x, (DK_B, LANES)), 0.0), axis=-1, keepdims=True)

    for h in range(N_HEADS_B):
        sl = slice(h * LANES, (h + 1) * LANES)
        q = _l2n(y[:, h * LANES:(h + 1) * LANES]) * (DK_B ** -0.5)
        k = _l2n(y[:, B_W + h * LANES:B_W + (h + 1) * LANES])
        v = y[:, 2 * B_W + h * LANES:2 * B_W + (h + 1) * LANES]
        a = jnp.exp(gb[:, h:h + 1])
        beta = gb[:, N_HEADS_B + h:N_HEADS_B + h + 1]
        s_old = s0_ref[0, h]
        k_col = column(k)
        q_col = column(q)
        v_new = beta * (v - a * jnp.sum(k_col * s_old, axis=0, keepdims=True))
        s_new = a * s_old + k_col * v_new
        s_ref[0, h] = s_new
        o = jnp.sum(q_col * s_new, axis=0, keepdims=True)
        on = o * lax.rsqrt(jnp.mean(o * o, axis=-1, keepdims=True) + EPS) * onw_ref[...]
        ob_ref[0, :, sl] = (on * zb_ref[0, :, sl]).astype(BF16)


def _delta_sample(u, conv_buf, conv_w, gb, zb, o_norm_w, s0):
    nb, _, cdim = u.shape
    row = lambda wd: pl.BlockSpec((1, 1, wd), lambda b: (b, 0, 0))
    st = pl.BlockSpec((1, N_HEADS_B, DK_B, DV_B), lambda b: (b, 0, 0, 0))
    return pl.pallas_call(
        _dec_delta_kernel,
        grid=(nb,),
        in_specs=[row(cdim), pl.BlockSpec((1, CONV_W - 1, cdim), lambda b: (b, 0, 0)),
                  pl.BlockSpec((CONV_W, cdim), lambda b: (0, 0)), row(LANES), row(B_W),
                  pl.BlockSpec((1, LANES), lambda b: (0, 0)), st],
        out_specs=[row(B_W), st],
        out_shape=[jax.ShapeDtypeStruct((nb, 1, B_W), BF16),
                   jax.ShapeDtypeStruct(s0.shape, F32)],
        compiler_params=_cparams(("parallel",)),
        name="delta_sample",
    )(u, conv_buf, conv_w, gb, zb, o_norm_w.reshape(1, LANES), s0)


def _out_kernel(oa_ref, za_ref, ob_ref, ga_ref, gbt_ref, x_ref, gate_ref, wa_ref, wb_ref, wo_ref, y_ref):
    ya = _dot((oa_ref[...] * za_ref[...]).astype(BF16), wa_ref[...])
    yb = _dot(ob_ref[...], wb_ref[...])
    mrg = ga_ref[...] * ya + gbt_ref[...] * yb
    y_ref[0] = x_ref[0] + gate_ref[0] * _dot(mrg.astype(BF16), wo_ref[...])


def _out_proj(oa, za, ob, gates, x, gate, wa, wb, wo, tm):
    b, s, d = x.shape
    nt = s // tm
    ts = tm if gate.shape[1] == s else 1
    gate_map = (lambda i, j: (i, j, 0)) if ts == tm else (lambda i, j: (i, 0, 0))
    rows = lambda wd, cb=0: pl.BlockSpec((tm, wd), lambda i, j, cb=cb: (i * nt + j, cb))
    const = lambda shp: pl.BlockSpec(shp, lambda i, j: (0, 0), pipeline_mode=pl.Buffered(1))
    return pl.pallas_call(
        _out_kernel,
        grid=(b, nt),
        in_specs=[rows(A_W), rows(A_W), rows(B_W), rows(d, 0), rows(d, 1),
                  pl.BlockSpec((1, tm, d), lambda i, j: (i, j, 0)),
                  pl.BlockSpec((1, ts, d), gate_map),
                  const(wa.shape), const(wb.shape), const(wo.shape)],
        out_specs=pl.BlockSpec((1, tm, d), lambda i, j: (i, j, 0)),
        out_shape=jax.ShapeDtypeStruct((b, s, d), F32),
        compiler_params=_cparams(("parallel", "parallel")),
        name="out_proj",
    )(oa, za, ob, gates, gates, x, gate, wa, wb, wo)


def _tile(n, pref):
    t = min(n, pref)
    while n % t:
        t //= 2
    return t


def _project(h2, wts, tm, kt_dims=None):
    (w_q, w_k, w_v, w_za, w_qkvb, w_zb, w_ab, w_g, gmat, qnw, knw, alog, dtb) = wts
    tn = A_W
    (q16,) = _proj("q", h2, w_q, [gmat, qnw], [BF16], tm, tn)
    if kt_dims is None:
        k32, k16 = _proj("k", h2, w_k, [gmat, knw], [F32, BF16], tm, tn)
    else:
        knw_col = knw[0, :HEAD_DIM_A].reshape(1, HEAD_DIM_A, 1)
        k32, k16 = _proj_kt(h2, w_k.T, knw_col, kt_dims[0], kt_dims[1], tm)
    v32, v16 = _proj("v", h2, w_v, [], [F32, BF16], tm, tn)
    (za,) = _proj("silu", h2, w_za, [], [F32], tm, tn)
    (raw,) = _proj("raw", h2, w_qkvb, [], [F32], tm, tn)
    (zb,) = _proj("silu", h2, w_zb, [], [F32], tm, tn)
    (gb,) = _proj("ab", h2, w_ab, [alog, dtb], [F32], tm, LANES)
    (gates,) = _proj("sigmoid", h2, w_g, [], [F32], tm, tn)
    return q16, k32, k16, v32, v16, za, raw, zb, gb, gates


def kernel(x_prompt, x_sample, c_prompt, c_sample, cache_k, cache_v, page_table, state_ssm, state_conv, w_ada, b_ada, norm_w, w_in, q_norm_w, k_norm_w, lam_q1, lam_k1, lam_q2, lam_k2, subln_w, conv_w, a_log, dt_bias, o_norm_w, w_branch_a, w_branch_b, w_out):
    depth = w_ada.shape[0]
    bsz, s_len, d = x_prompt.shape
    nb, t_dec, _ = x_sample.shape
    assert t_dec == 1 and s_len % CHUNK == 0 and s_len >= CONV_W - 1
    m_p = bsz * s_len
    cdim = conv_w.shape[-1]
    o_q, o_k, o_v, o_za = 0, A_W, 2 * A_W, 3 * A_W
    o_qkvb = 4 * A_W
    o_zb = o_qkvb + cdim
    o_ab = o_zb + B_W
    o_g = o_ab + 2 * N_HEADS_B

    gi = jnp.arange(LANES) // HEAD_DIM_A
    gmat = jnp.where(gi[:, None] == gi[None, :], 1.0 / HEAD_DIM_A, 0.0).astype(BF16)

    yp, ys = x_prompt, x_sample
    outs = [[] for _ in range(8)]
    for li in range(depth):
        lambda_init = 0.8 - 0.6 * math.exp(-0.3 * li)
        wl = w_in[li]
        wb16 = lambda a, b: wl[:, a:b].astype(BF16)
        w_abp = jnp.pad(wl[:, o_ab:o_g], ((0, 0), (0, LANES - 2 * N_HEADS_B))).astype(BF16)
        pad8 = lambda vec: jnp.pad(vec, (0, LANES - N_HEADS_B)).reshape(1, LANES)
        wts = (wb16(o_q, o_k), wb16(o_k, o_v), wb16(o_v, o_za), wb16(o_za, o_qkvb), wb16(o_qkvb, o_zb),
               wb16(o_zb, o_ab), w_abp, wb16(o_g, o_g + 2 * d), gmat,
               jnp.tile(q_norm_w[li], 2 * N_HEADS_A).reshape(1, A_W),
               jnp.tile(k_norm_w[li], 2 * N_HEADS_A).reshape(1, A_W),
               pad8(a_log[li]), pad8(dt_bias[li]))
        wa16 = w_branch_a[li].astype(BF16)
        wbb16 = w_branch_b[li].astype(BF16)
        wo16 = w_out[li].astype(BF16)
        lamp = jnp.stack([lam_q1[li], lam_k1[li], lam_q2[li], lam_k2[li]])

        n_c = bsz + nb
        c_all = jnp.pad(jnp.concatenate([c_prompt, c_sample], axis=0), ((0, (-n_c) % 16), (0, 0)))
        mod = _ada(c_all, w_ada[li], b_ada[li])
        shift, scale, gate = mod[:, :d], mod[:, d:2 * d], mod[:, 2 * d:]

        tm = _tile(s_len, 1024)
        h = _hnorm(yp, shift[:bsz, None], scale[:bsz, None], norm_w[li], tm)
        q16, kt32, kt16, v32, v16, za, raw, zb, gb, gates = _project(h.reshape(m_p, d), wts, tm, (bsz, s_len))
        oa = _attn_prompt(q16, kt16, v16, lamp, subln_w[li], bsz, s_len, _tile(s_len, 256), lambda_init)
        qkv = _conv_prompt(raw, conv_w[li], bsz, s_len, _tile(s_len, 512))
        ob, ssm_p = _delta_prompt(qkv, gb, zb, o_norm_w[li], bsz, s_len)
        yp = _out_proj(oa, za, ob, gates, yp, gate[:bsz, None], wa16, wbb16, wo16, _tile(s_len, 256))
        outs[0].append(jnp.transpose(kt32.reshape(bsz, N_HEADS_A, 2, HEAD_DIM_A, s_len), (0, 4, 1, 2, 3)))
        outs[1].append(v32.reshape(bsz, s_len, N_HEADS_A, 2 * HEAD_DIM_A))
        outs[4].append(ssm_p)
        outs[6].append(raw.reshape(bsz, s_len, cdim)[:, s_len - (CONV_W - 1):])

        xs = ys.reshape(1, nb, d)
        sl_s = slice(bsz, bsz + nb)
        h = _hnorm(xs, shift[None, sl_s], scale[None, sl_s], norm_w[li], nb)
        q16, k32, k16, v32, v16, za, raw, zb, gb, gates = _project(h.reshape(nb, d), wts, nb)
        n_pool = cache_k.shape[1]
        cache_kt = jnp.transpose(cache_k[li], (0, 2, 3, 4, 1)).reshape(n_pool, A_W, PAGE_SIZE)
        oa = _attn_sample(q16.astype(F32).reshape(nb, 1, A_W), k32.reshape(nb, 1, A_W), v32.reshape(nb, 1, A_W),
                          cache_kt, cache_v[li].reshape(n_pool, PAGE_SIZE, A_W),
                          page_table, lamp, subln_w[li], lambda_init)
        ob, ssm_s = _delta_sample(raw.reshape(nb, 1, cdim), state_conv[li], conv_w[li], gb.reshape(nb, 1, LANES),
                                  zb.reshape(nb, 1, B_W), o_norm_w[li], state_ssm[li])
        ys = _out_proj(oa.reshape(nb, A_W), za, ob.reshape(nb, B_W), gates, xs, gate[None, sl_s],
                       wa16, wbb16, wo16, nb).reshape(nb, 1, d)
        outs[2].append(k32.reshape(nb, 1, N_HEADS_A, 2, HEAD_DIM_A))
        outs[3].append(v32.reshape(nb, 1, N_HEADS_A, 2 * HEAD_DIM_A))
        outs[5].append(ssm_s)
        outs[7].append(jnp.concatenate([state_conv[li][:, 1:], raw.reshape(nb, 1, cdim)], axis=1))

    st = [jnp.stack(o) for o in outs]
    return (yp, ys, st[0], st[1], st[2], st[3], st[4], st[5], st[6], st[7])
```

```python
import functools
import math

import jax
import jax.numpy as jnp
from jax import lax
from jax.experimental import pallas as pl
from jax.experimental.pallas import tpu as pltpu

F32 = jnp.float32
BF16 = jnp.bfloat16

N_HEADS_A = 8
HEAD_DIM_A = 64
A_W = N_HEADS_A * 2 * HEAD_DIM_A
N_HEADS_B = 8
DK_B = 128
DV_B = 128
B_W = N_HEADS_B * DK_B
CONV_W = 4
CHUNK = 64
PAGE_SIZE = 128
EPS = 1e-6
LANES = 128
SUBLANES = 8
VMEM_LIMIT = 56 * 1024 * 1024
PAGES_PER_STEP = 8

_NT = (((1,), (1,)), ((), ()))
_TN = (((0,), (0,)), ((), ()))


def _cparams(sem):
    return pltpu.CompilerParams(dimension_semantics=sem, vmem_limit_bytes=VMEM_LIMIT)


def _dot(a, b):
    return jnp.dot(a, b, preferred_element_type=F32)


def _dot_nt(a, b):
    return lax.dot_general(a, b, _NT, preferred_element_type=F32)


def _dot_tn(a, b):
    return lax.dot_general(a, b, _TN, preferred_element_type=F32)


def _sigmoid(x):
    return 1.0 / (1.0 + jnp.exp(-x))


def _silu(x):
    return x * _sigmoid(x)


def _ada_kernel(c_ref, w_ref, b_ref, o_ref):
    sc = _silu(c_ref[...]).astype(BF16)
    o_ref[...] = _dot(sc, w_ref[...].astype(BF16)) + b_ref[...]


def _ada(c, w_ada, b_ada):
    m, d = c.shape
    n = w_ada.shape[1]
    tn = 512
    return pl.pallas_call(
        _ada_kernel,
        grid=(n // tn,),
        in_specs=[pl.BlockSpec((m, d), lambda j: (0, 0)),
                  pl.BlockSpec((d, tn), lambda j: (0, j)),
                  pl.BlockSpec((1, tn), lambda j: (0, j))],
        out_specs=pl.BlockSpec((m, tn), lambda j: (0, j)),
        out_shape=jax.ShapeDtypeStruct((m, n), F32),
        compiler_params=_cparams(("parallel",)),
        name="ada",
    )(c, w_ada, b_ada.reshape(1, n))


def _hnorm_kernel(x_ref, shift_ref, scale_ref, nw_ref, h_ref):
    x = x_ref[0]
    ms = jnp.mean(x * x, axis=-1, keepdims=True)
    y = x * lax.rsqrt(ms + EPS) * nw_ref[...]
    h_ref[0] = (y * (1.0 + scale_ref[0]) + shift_ref[0]).astype(BF16)


def _hnorm(x, shift, scale, norm_w, tm):
    b, s, d = x.shape
    ts = tm if shift.shape[1] == s else 1
    mod_map = (lambda i, j: (i, j, 0)) if ts == tm else (lambda i, j: (i, 0, 0))
    return pl.pallas_call(
        _hnorm_kernel,
        grid=(b, s // tm),
        in_specs=[pl.BlockSpec((1, tm, d), lambda i, j: (i, j, 0)),
                  pl.BlockSpec((1, ts, d), mod_map),
                  pl.BlockSpec((1, ts, d), mod_map),
                  pl.BlockSpec((1, d), lambda i, j: (0, 0))],
        out_specs=pl.BlockSpec((1, tm, d), lambda i, j: (i, j, 0)),
        out_shape=jax.ShapeDtypeStruct((b, s, d), BF16),
        compiler_params=_cparams(("parallel", "parallel")),
        name="hnorm",
    )(x, shift, scale, norm_w.reshape(1, d))


def _group_rms(acc, g_ref):
    sq = acc * acc
    hi = sq.astype(BF16)
    lo = (sq - hi.astype(F32)).astype(BF16)
    g = g_ref[...]
    outs = []
    for hh in range(acc.shape[1] // LANES):
        sl = slice(hh * LANES, (hh + 1) * LANES)
        ms = _dot(hi[:, sl], g) + _dot(lo[:, sl], g)
        outs.append(acc[:, sl] * lax.rsqrt(ms + EPS))
    return outs


def _proj_kernel(kind, h_ref, w_ref, *refs):
    acc = _dot(h_ref[...], w_ref[...])
    if kind == "q":
        g_ref, nw_ref, o_ref = refs
        for hh, y in enumerate(_group_rms(acc, g_ref)):
            sl = slice(hh * LANES, (hh + 1) * LANES)
            o_ref[:, sl] = (y * nw_ref[:, sl] * (HEAD_DIM_A ** -0.5)).astype(BF16)
    elif kind == "k":
        g_ref, nw_ref, o_ref, ob_ref = refs
        for hh, y in enumerate(_group_rms(acc, g_ref)):
            sl = slice(hh * LANES, (hh + 1) * LANES)
            y = y * nw_ref[:, sl]
            o_ref[:, sl] = y
            ob_ref[:, sl] = y.astype(BF16)
    elif kind == "v":
        o_ref, ob_ref = refs
        o_ref[...] = acc
        ob_ref[...] = acc.astype(BF16)
    elif kind == "silu":
        (o_ref,) = refs
        o_ref[...] = _silu(acc)
    elif kind == "sigmoid":
        (o_ref,) = refs
        o_ref[...] = _sigmoid(acc)
    elif kind == "raw":
        (o_ref,) = refs
        o_ref[...] = acc
    elif kind == "ab":
        alog_ref, dtb_ref, o_ref = refs
        a = acc + dtb_ref[...]
        sp = jnp.maximum(a, 0.0) + jnp.log(1.0 + jnp.exp(-jnp.abs(a)))
        g = -jnp.exp(alog_ref[...]) * sp
        lane = lax.broadcasted_iota(jnp.int32, acc.shape, 1)
        o_ref[...] = jnp.where(lane < N_HEADS_B, g, _sigmoid(acc))
    else:
        raise ValueError(kind)


def _proj_kt_kernel(h_ref, wt_ref, nw_ref, o_ref, ob_ref):
    acc = _dot_nt(wt_ref[...], h_ref[...])
    tm = acc.shape[1]
    x3 = acc.reshape(2 * N_HEADS_A, HEAD_DIM_A, tm)
    ms = jnp.mean(x3 * x3, axis=1, keepdims=True)
    y = (x3 * lax.rsqrt(ms + EPS) * nw_ref[...]).reshape(A_W, tm)
    o_ref[0] = y
    ob_ref[0] = y.astype(BF16)


def _proj_kt(h, wt, nw_col, bsz, s_len, tm):
    k = h.shape[1]
    nt = s_len // tm
    out_spec = pl.BlockSpec((1, A_W, tm), lambda i: (i // nt, 0, i % nt))
    return pl.pallas_call(
        _proj_kt_kernel,
        grid=(bsz * nt,),
        in_specs=[pl.BlockSpec((tm, k), lambda i: (i, 0)),
                  pl.BlockSpec((A_W, k), lambda i: (0, 0)),
                  pl.BlockSpec(nw_col.shape, lambda i: (0, 0, 0))],
        out_specs=[out_spec, out_spec],
        out_shape=[jax.ShapeDtypeStruct((bsz, A_W, s_len), F32),
                   jax.ShapeDtypeStruct((bsz, A_W, s_len), BF16)],
        compiler_params=_cparams(("parallel",)),
        name="proj_kt",
    )(h, wt, nw_col)


def _proj(kind, h, w, extras, out_dtypes, tm, tn):
    m, k = h.shape
    n = w.shape[1]
    extra_specs = [pl.BlockSpec(e.shape, lambda j, i, nd=e.ndim: (0,) * nd) for e in extras]
    if kind in ("q", "k"):
        extra_specs[1] = pl.BlockSpec((1, tn), lambda j, i: (0, j))
    outs = pl.pallas_call(
        functools.partial(_proj_kernel, kind),
        grid=(n // tn, m // tm),
        in_specs=[pl.BlockSpec((tm, k), lambda j, i: (i, 0)),
                  pl.BlockSpec((k, tn), lambda j, i: (0, j))] + extra_specs,
        out_specs=[pl.BlockSpec((tm, tn), lambda j, i: (i, j)) for _ in out_dtypes],
        out_shape=[jax.ShapeDtypeStruct((m, n), dt) for dt in out_dtypes],
        compiler_params=_cparams(("parallel", "parallel")),
        name="proj_" + kind,
    )(h, w, *extras)
    return outs


def _lam(lamp_ref, lambda_init):
    lp = lamp_ref[...]
    return (jnp.exp(jnp.sum(lp[0:1] * lp[1:2], axis=-1, keepdims=True))
            - jnp.exp(jnp.sum(lp[2:3] * lp[3:4], axis=-1, keepdims=True)) + lambda_init)


def _subln(o, sw, lambda_init):
    ms = jnp.mean(o * o, axis=-1, keepdims=True)
    return o * lax.rsqrt(ms + EPS) * sw * (1.0 - lambda_init)


def _attn_kernel(lamp_ref, q_ref, k_ref, v_ref, sw_ref, o_ref, m_sc, l_sc, acc_sc, *, tq, lambda_init):
    i = pl.program_id(2)
    q = q_ref[...]
    lane = lax.broadcasted_iota(jnp.int32, q.shape, 1)
    zero = jnp.zeros_like(q)
    qs = (jnp.where(lane < HEAD_DIM_A, q, zero), jnp.where(lane >= HEAD_DIM_A, q, zero))
    m_sc[...] = jnp.full(m_sc.shape, -jnp.inf, F32)
    l_sc[...] = jnp.zeros(l_sc.shape, F32)
    acc_sc[...] = jnp.zeros(acc_sc.shape, F32)

    def step(j, masked):
        start = pl.multiple_of(j * tq, tq)
        kb = k_ref[0, :, pl.ds(start, tq)]
        vb = v_ref[pl.ds(start, tq), :]
        for c in range(2):
            s = _dot(qs[c], kb)
            if masked:
                row = lax.broadcasted_iota(jnp.int32, s.shape, 0)
                col = lax.broadcasted_iota(jnp.int32, s.shape, 1)
                s = jnp.where(col <= row, s, -jnp.inf)
            m_prev = m_sc[c]
            m_new = jnp.maximum(m_prev, jnp.max(s, axis=-1, keepdims=True))
            alpha = jnp.exp(m_prev - m_new)
            ps = [jnp.exp(s[:, g * LANES:(g + 1) * LANES] - m_new) for g in range(tq // LANES)]
            l_sc[c] = alpha * l_sc[c] + functools.reduce(lambda a, b: a + b, ps)
            p = jnp.concatenate([x.astype(BF16) for x in ps], axis=1)
            acc_sc[c] = alpha * acc_sc[c] + _dot(p, vb)
            m_sc[c] = m_new

    def body(j, carry):
        step(j, False)
        return carry

    lax.fori_loop(0, i, body, 0)
    step(i, True)
    lam = _lam(lamp_ref, lambda_init)
    l0 = jnp.sum(l_sc[0], axis=-1, keepdims=True)
    l1 = jnp.sum(l_sc[1], axis=-1, keepdims=True)
    o = acc_sc[0] / l0 - lam * (acc_sc[1] / l1)
    o_ref[...] = _subln(o, sw_ref[...], lambda_init)


def _attn_prompt(q, k, v, lamp, subln_w, bsz, s_len, tq, lambda_init):
    m = q.shape[0]
    nq = s_len // tq
    return pl.pallas_call(
        functools.partial(_attn_kernel, tq=tq, lambda_init=lambda_init),
        grid=(bsz, N_HEADS_A, nq),
        in_specs=[pl.BlockSpec(lamp.shape, lambda b, h, i: (0, 0)),
                  pl.BlockSpec((tq, LANES), lambda b, h, i: (b * nq + i, h)),
                  pl.BlockSpec((1, LANES, s_len), lambda b, h, i: (b, h, 0)),
                  pl.BlockSpec((s_len, LANES), lambda b, h, i: (b, h)),
                  pl.BlockSpec((1, LANES), lambda b, h, i: (0, 0))],
        out_specs=pl.BlockSpec((tq, LANES), lambda b, h, i: (b * nq + i, h)),
        out_shape=jax.ShapeDtypeStruct((m, A_W), F32),
        scratch_shapes=[pltpu.VMEM((2, tq, LANES), F32)] * 3,
        compiler_params=_cparams(("parallel", "parallel", "parallel")),
        name="attn_prompt",
    )(lamp, q, k, v, subln_w.reshape(1, LANES))


def _dec_attn_kernel(pt_ref, lamp_ref, q_ref, kn_ref, vn_ref, sw_ref, e_ref, *refs, n_pages_step, lambda_init):
    k_refs = refs[:n_pages_step]
    v_refs = refs[n_pages_step:2 * n_pages_step]
    o_ref, m_sc, l_sc, acc_sc = refs[2 * n_pages_step:]
    j = pl.program_id(1)
    nrow = 2 * N_HEADS_A
    q = q_ref[0]
    hc = lax.broadcasted_iota(jnp.int32, (nrow, A_W), 0)
    lane = lax.broadcasted_iota(jnp.int32, (nrow, A_W), 1)
    qbd = jnp.where(lane // HEAD_DIM_A == hc, jnp.broadcast_to(q, (nrow, A_W)), 0.0).astype(BF16)
    own_head = lane % N_HEADS_A == hc // 2

    @pl.when(j == 0)
    def _():
        kn = jnp.broadcast_to(kn_ref[0], (nrow, A_W)).astype(BF16).astype(F32)
        m_sc[...] = jnp.sum(qbd.astype(F32) * kn, axis=-1, keepdims=True)
        l_sc[...] = jnp.ones(l_sc.shape, F32)
        vn = vn_ref[0].astype(BF16).astype(F32)
        row_head = lax.broadcasted_iota(jnp.int32, (nrow, LANES), 0) // 2
        acc = jnp.zeros((nrow, LANES), F32)
        for h in range(N_HEADS_A):
            acc = jnp.where(row_head == h, jnp.broadcast_to(vn[:, h * LANES:(h + 1) * LANES], (nrow, LANES)), acc)
        acc_sc[...] = acc

    s = jnp.concatenate([_dot(qbd, k_refs[p_i][0].astype(BF16)) for p_i in range(n_pages_step)], axis=1)
    m_prev = m_sc[...]
    m_new = jnp.maximum(m_prev, jnp.max(s, axis=-1, keepdims=True))
    alpha = jnp.exp(m_prev - m_new)
    p = jnp.exp(s - m_new)
    l_sc[...] = alpha * l_sc[...] + jnp.sum(p, axis=-1, keepdims=True)
    p16 = p.astype(BF16)
    pv = jnp.zeros((nrow, LANES), F32)
    for p_i in range(n_pages_step):
        pe = _dot(p16[:, p_i * PAGE_SIZE:(p_i + 1) * PAGE_SIZE], e_ref[...])
        pe = jnp.where(own_head, pe, 0.0).astype(BF16)
        v2 = v_refs[p_i][0].reshape(PAGE_SIZE * N_HEADS_A, LANES).astype(BF16)
        pv = pv + _dot(pe, v2)
    acc_sc[...] = alpha * acc_sc[...] + pv
    m_sc[...] = m_new

    @pl.when(j == pl.num_programs(1) - 1)
    def _():
        lam = _lam(lamp_ref, lambda_init)
        o = acc_sc[...] / l_sc[...]
        for h in range(N_HEADS_A):
            oh = o[2 * h:2 * h + 1] - lam * o[2 * h + 1:2 * h + 2]
            o_ref[0, :, h * LANES:(h + 1) * LANES] = _subln(oh, sw_ref[...], lambda_init)


def _attn_sample(q, kn, vn, cache_kt, cache_v, page_table, lamp, subln_w, lambda_init):
    nb, n_pages = page_table.shape
    pps = PAGES_PER_STEP if n_pages % PAGES_PER_STEP == 0 else 1
    row_spec = pl.BlockSpec((1, 1, A_W), lambda b, j, pt: (b, 0, 0))

    def page_spec(p_i, shape):
        return pl.BlockSpec((1,) + shape, lambda b, j, pt: (pt[b, j * pps + p_i],) + (0,) * len(shape))

    expand = (jnp.arange(PAGE_SIZE)[:, None] == jnp.arange(PAGE_SIZE * N_HEADS_A)[None, :] // N_HEADS_A).astype(BF16)
    grid_spec = pltpu.PrefetchScalarGridSpec(
        num_scalar_prefetch=1,
        grid=(nb, n_pages // pps),
        in_specs=[pl.BlockSpec(lamp.shape, lambda b, j, pt: (0, 0)), row_spec, row_spec, row_spec,
                  pl.BlockSpec((1, LANES), lambda b, j, pt: (0, 0)),
                  pl.BlockSpec(expand.shape, lambda b, j, pt: (0, 0))]
                 + [page_spec(p_i, (A_W, PAGE_SIZE)) for p_i in range(pps)]
                 + [page_spec(p_i, (PAGE_SIZE, N_HEADS_A, LANES)) for p_i in range(pps)],
        out_specs=row_spec,
        scratch_shapes=[pltpu.VMEM((2 * N_HEADS_A, 1), F32), pltpu.VMEM((2 * N_HEADS_A, 1), F32),
                        pltpu.VMEM((2 * N_HEADS_A, LANES), F32)],
    )
    return pl.pallas_call(
        functools.partial(_dec_attn_kernel, n_pages_step=pps, lambda_init=lambda_init),
        grid_spec=grid_spec,
        out_shape=jax.ShapeDtypeStruct((nb, 1, A_W), F32),
        compiler_params=_cparams(("parallel", "arbitrary")),
        name="attn_sample",
    )(page_table, lamp, q, kn, vn, subln_w.reshape(1, LANES), expand,
      *([cache_kt] * pps), *([cache_v] * pps))


def _l2n(y):
    return y * lax.rsqrt(jnp.sum(y * y, axis=-1, keepdims=True) + EPS)


def _conv_kernel(x_ref, w_ref, o_ref, ext_ref, *, tm):
    seg = pl.program_id(1)
    t = pl.program_id(2)

    @pl.when(t == 0)
    def _():
        ext_ref[0:SUBLANES, :] = jnp.zeros((SUBLANES, ext_ref.shape[1]), F32)

    ext_ref[SUBLANES:SUBLANES + tm, :] = x_ref[...]
    w = w_ref[...]
    y = ext_ref[SUBLANES - 3:SUBLANES - 3 + tm, :] * w[0:1]
    for jj in range(1, CONV_W):
        off = SUBLANES - 3 + jj
        y = y + ext_ref[off:off + tm, :] * w[jj:jj + 1]
    y = _silu(y)
    qscale = jnp.where(seg == 0, DK_B ** -0.5, 1.0)
    for h in range(N_HEADS_B):
        sl = slice(h * LANES, (h + 1) * LANES)
        yh = y[:, sl]
        o_ref[:, sl] = jnp.where(seg < 2, _l2n(yh) * qscale, yh)
    ext_ref[0:SUBLANES, :] = ext_ref[tm:tm + SUBLANES, :]


def _conv_prompt(raw, conv_w, bsz, s_len, tm):
    m, cdim = raw.shape
    nt = s_len // tm
    return pl.pallas_call(
        functools.partial(_conv_kernel, tm=tm),
        grid=(bsz, cdim // B_W, nt),
        in_specs=[pl.BlockSpec((tm, B_W), lambda b, g, t: (b * nt + t, g)),
                  pl.BlockSpec((CONV_W, B_W), lambda b, g, t: (0, g))],
        out_specs=pl.BlockSpec((tm, B_W), lambda b, g, t: (b * nt + t, g)),
        out_shape=jax.ShapeDtypeStruct((m, cdim), F32),
        scratch_shapes=[pltpu.VMEM((tm + SUBLANES, B_W), F32)],
        compiler_params=_cparams(("parallel", "parallel", "arbitrary")),
        name="conv_prompt",
    )(raw, conv_w)


def _bdot(a, b):
    return lax.dot_general(a, b, (((2,), (1,)), ((0,), (0,))), preferred_element_type=F32)


def _bdot_nt(a, b):
    return lax.dot_general(a, b, (((2,), (2,)), ((0,), (0,))), preferred_element_type=F32)


def _bdot_tn(a, b):
    return lax.dot_general(a, b, (((1,), (1,)), ((0,), (0,))), preferred_element_type=F32)


def _split_bdot(a, b):
    ah = a.astype(BF16)
    al = (a - ah.astype(F32)).astype(BF16)
    bh = b.astype(BF16)
    bl = (b - bh.astype(F32)).astype(BF16)
    return _bdot(ah, bh) + _bdot(ah, bl) + _bdot(al, bh)


def _cumsum_rows(x):
    n = x.shape[0]
    row = lax.broadcasted_iota(jnp.int32, x.shape, 0)
    s = 1
    while s < n:
        x = x + jnp.where(row >= s, pltpu.roll(x, s, axis=0), 0.0)
        s *= 2
    return x


def _delta_kernel(q_ref, k_ref, v_ref, gb_ref, zb_ref, onw_ref, ob_ref, s_ref):
    c = pl.program_id(1)

    @pl.when(c == 0)
    def _():
        s_ref[...] = jnp.zeros(s_ref.shape, F32)

    nh = N_HEADS_B
    heads = lambda ref: jnp.stack([ref[:, h * LANES:(h + 1) * LANES] for h in range(nh)])
    lane_bcast = lambda x, off: jnp.stack(
        [jnp.broadcast_to(x[:, off + h:off + h + 1], (CHUNK, LANES)) for h in range(nh)])
    q, k, v = heads(q_ref), heads(k_ref), heads(v_ref)
    gb = gb_ref[...]
    gc_all = _cumsum_rows(gb)
    gc_t = gc_all.T
    gc = lane_bcast(gc_all, 0)
    gc_row = jnp.stack([gc_t[h:h + 1, :] for h in range(nh)])
    beta = lane_bcast(gb, nh)
    row = lax.broadcasted_iota(jnp.int32, (1, CHUNK, CHUNK), 1)
    col = lax.broadcasted_iota(jnp.int32, (1, CHUNK, CHUNK), 2)
    tril = col <= row
    strict = col < row
    eye = jnp.where(row == col, 1.0, 0.0)
    decay = jnp.exp(jnp.where(tril, gc[:, :, :CHUNK] - gc_row, -jnp.inf))
    kb = k * beta
    k16 = k.astype(BF16)
    lmat = jnp.where(strict, _bdot_nt(kb.astype(BF16), k16) * decay, 0.0)
    tinv = eye - lmat
    pw = lmat
    for _ in range(5):
        pw = _split_bdot(pw, pw)
        tinv = tinv + _split_bdot(tinv, pw)
    t16 = tinv.astype(BF16)
    u = _bdot(t16, (v * beta).astype(BF16))
    wk = _bdot(t16, (kb * jnp.exp(gc)).astype(BF16))
    qk = jnp.where(tril, _bdot_nt(q.astype(BF16), k16) * decay, 0.0)
    s_old = s_ref[0]
    s16 = s_old.astype(BF16)
    v_new = u - _bdot(wk.astype(BF16), s16)
    vn16 = v_new.astype(BF16)
    o = _bdot((q * jnp.exp(gc)).astype(BF16), s16) + _bdot(qk.astype(BF16), vn16)
    g_last = gc[:, CHUNK - 1:CHUNK, :]
    k_dec = k * jnp.exp(g_last - gc)
    s_ref[0] = s_old * jnp.exp(g_last) + _bdot_tn(k_dec.astype(BF16), vn16)
    on = o * lax.rsqrt(jnp.mean(o * o, axis=-1, keepdims=True) + EPS) * onw_ref[...]
    for h in range(nh):
        sl = slice(h * LANES, (h + 1) * LANES)
        ob_ref[:, sl] = (on[h] * zb_ref[:, sl]).astype(BF16)


def _delta_prompt(qkv, gb, zb, o_norm_w, bsz, s_len):
    m = qkv.shape[0]
    nc = s_len // CHUNK
    seg = lambda g: pl.BlockSpec((CHUNK, B_W), lambda b, c, g=g: (b * nc + c, g))
    return pl.pallas_call(
        _delta_kernel,
        grid=(bsz, nc),
        in_specs=[seg(0), seg(1), seg(2),
                  pl.BlockSpec((CHUNK, LANES), lambda b, c: (b * nc + c, 0)),
                  seg(0),
                  pl.BlockSpec((1, LANES), lambda b, c: (0, 0))],
        out_specs=[seg(0), pl.BlockSpec((1, N_HEADS_B, DK_B, DV_B), lambda b, c: (b, 0, 0, 0))],
        out_shape=[jax.ShapeDtypeStruct((m, B_W), BF16),
                   jax.ShapeDtypeStruct((bsz, N_HEADS_B, DK_B, DV_B), F32)],
        compiler_params=_cparams(("parallel", "arbitrary")),
        name="delta_prompt",
    )(qkv, qkv, qkv, gb, zb, o_norm_w.reshape(1, LANES))


def _dec_delta_kernel(u_ref, buf_ref, w_ref, gb_ref, zb_ref, onw_ref, s0_ref, ob_ref, s_ref):
    w = w_ref[...]
    buf = buf_ref[0]
    y = buf[0:1] * w[0:1] + buf[1:2] * w[1:2] + buf[2:3] * w[2:3] + u_ref[0] * w[3:4]
    y = _silu(y)
    gb = gb_ref[0]
    r = lax.broadcasted_iota(jnp.int32, (DK_B, LANES), 0)
    cidx = lax.broadcasted_iota(jnp.int32, (DK_B, LANES), 1)
    eye = r == cidx

    def column(x):
        return jnp.sum(jnp.where(eye, jnp.broadcast_to(x, (DK_B, LANES)), 0.0), axis=-1, keepdims=True)

    for h in range(N_HEADS_B):
        sl = slice(h * LANES, (h + 1) * LANES)
        q = _l2n(y[:, h * LANES:(h + 1) * LANES]) * (DK_B ** -0.5)
        k = _l2n(y[:, B_W + h * LANES:B_W + (h + 1) * LANES])
        v = y[:, 2 * B_W + h * LANES:2 * B_W + (h + 1) * LANES]
        a = jnp.exp(gb[:, h:h + 1])
        beta = gb[:, N_HEADS_B + h:N_HEADS_B + h + 1]
        s_old = s0_ref[0, h]
        k_col = column(k)
        q_col = column(q)
        v_new = beta * (v - a * jnp.sum(k_col * s_old, axis=0, keepdims=True))
        s_new = a * s_old + k_col * v_new
        s_ref[0, h] = s_new
        o = jnp.sum(q_col * s_new, axis=0, keepdims=True)
        on = o * lax.rsqrt(jnp.mean(o * o, axis=-1, keepdims=True) + EPS) * onw_ref[...]
        ob_ref[0, :, sl] = (on * zb_ref[0, :, sl]).astype(BF16)


def _delta_sample(u, conv_buf, conv_w, gb, zb, o_norm_w, s0):
    nb, _, cdim = u.shape
    row = lambda wd: pl.BlockSpec((1, 1, wd), lambda b: (b, 0, 0))
    st = pl.BlockSpec((1, N_HEADS_B, DK_B, DV_B), lambda b: (b, 0, 0, 0))
    return pl.pallas_call(
        _dec_delta_kernel,
        grid=(nb,),
        in_specs=[row(cdim), pl.BlockSpec((1, CONV_W - 1, cdim), lambda b: (b, 0, 0)),
                  pl.BlockSpec((CONV_W, cdim), lambda b: (0, 0)), row(LANES), row(B_W),
                  pl.BlockSpec((1, LANES), lambda b: (0, 0)), st],
        out_specs=[row(B_W), st],
        out_shape=[jax.ShapeDtypeStruct((nb, 1, B_W), BF16),
                   jax.ShapeDtypeStruct(s0.shape, F32)],
        compiler_params=_cparams(("parallel",)),
        name="delta_sample",
    )(u, conv_buf, conv_w, gb, zb, o_norm_w.reshape(1, LANES), s0)


def _out_kernel(oa_ref, za_ref, ob_ref, ga_ref, gbt_ref, x_ref, gate_ref, wa_ref, wb_ref, wo_ref, y_ref):
    ya = _dot((oa_ref[...] * za_ref[...]).astype(BF16), wa_ref[...])
    yb = _dot(ob_ref[...], wb_ref[...])
    mrg = ga_ref[...] * ya + gbt_ref[...] * yb
    y_ref[0] = x_ref[0] + gate_ref[0] * _dot(mrg.astype(BF16), wo_ref[...])


def _out_proj(oa, za, ob, gates, x, gate, wa, wb, wo, tm):
    b, s, d = x.shape
    nt = s // tm
    ts = tm if gate.shape[1] == s else 1
    gate_map = (lambda i, j: (i, j, 0)) if ts == tm else (lambda i, j: (i, 0, 0))
    rows = lambda wd, cb=0: pl.BlockSpec((tm, wd), lambda i, j, cb=cb: (i * nt + j, cb))
    const = lambda shp: pl.BlockSpec(shp, lambda i, j: (0, 0), pipeline_mode=pl.Buffered(1))
    return pl.pallas_call(
        _out_kernel,
        grid=(b, nt),
        in_specs=[rows(A_W), rows(A_W), rows(B_W), rows(d, 0), rows(d, 1),
                  pl.BlockSpec((1, tm, d), lambda i, j: (i, j, 0)),
                  pl.BlockSpec((1, ts, d), gate_map),
                  const(wa.shape), const(wb.shape), const(wo.shape)],
        out_specs=pl.BlockSpec((1, tm, d), lambda i, j: (i, j, 0)),
        out_shape=jax.ShapeDtypeStruct((b, s, d), F32),
        compiler_params=_cparams(("parallel", "parallel")),
        name="out_proj",
    )(oa, za, ob, gates, gates, x, gate, wa, wb, wo)


def _tile(n, pref):
    t = min(n, pref)
    while n % t:
        t //= 2
    return t


def _project(h2, wts, tm, kt_dims=None):
    (w_q, w_k, w_v, w_za, w_qkvb, w_zb, w_ab, w_g, gmat, qnw, knw, alog, dtb) = wts
    tn = A_W
    (q16,) = _proj("q", h2, w_q, [gmat, qnw], [BF16], tm, tn)
    if kt_dims is None:
        k32, k16 = _proj("k", h2, w_k, [gmat, knw], [F32, BF16], tm, tn)
    else:
        knw_col = knw[0, :HEAD_DIM_A].reshape(1, HEAD_DIM_A, 1)
        k32, k16 = _proj_kt(h2, w_k.T, knw_col, kt_dims[0], kt_dims[1], tm)
    v32, v16 = _proj("v", h2, w_v, [], [F32, BF16], tm, tn)
    (za,) = _proj("silu", h2, w_za, [], [F32], tm, tn)
    (raw,) = _proj("raw", h2, w_qkvb, [], [F32], tm, tn)
    (zb,) = _proj("silu", h2, w_zb, [], [F32], tm, tn)
    (gb,) = _proj("ab", h2, w_ab, [alog, dtb], [F32], tm, LANES)
    (gates,) = _proj("sigmoid", h2, w_g, [], [F32], tm, tn)
    return q16, k32, k16, v32, v16, za, raw, zb, gb, gates


def kernel(x_prompt, x_sample, c_prompt, c_sample, cache_k, cache_v, page_table, state_ssm, state_conv, w_ada, b_ada, norm_w, w_in, q_norm_w, k_norm_w, lam_q1, lam_k1, lam_q2, lam_k2, subln_w, conv_w, a_log, dt_bias, o_norm_w, w_branch_a, w_branch_b, w_out):
    depth = w_ada.shape[0]
    bsz, s_len, d = x_prompt.shape
    nb, t_dec, _ = x_sample.shape
    assert t_dec == 1 and s_len % CHUNK == 0 and s_len >= CONV_W - 1
    m_p = bsz * s_len
    cdim = conv_w.shape[-1]
    o_q, o_k, o_v, o_za = 0, A_W, 2 * A_W, 3 * A_W
    o_qkvb = 4 * A_W
    o_zb = o_qkvb + cdim
    o_ab = o_zb + B_W
    o_g = o_ab + 2 * N_HEADS_B

    gi = jnp.arange(LANES) // HEAD_DIM_A
    gmat = jnp.where(gi[:, None] == gi[None, :], 1.0 / HEAD_DIM_A, 0.0).astype(BF16)

    yp, ys = x_prompt, x_sample
    outs = [[] for _ in range(8)]
    for li in range(depth):
        lambda_init = 0.8 - 0.6 * math.exp(-0.3 * li)
        wl = w_in[li]
        wb16 = lambda a, b: wl[:, a:b].astype(BF16)
        w_abp = jnp.pad(wl[:, o_ab:o_g], ((0, 0), (0, LANES - 2 * N_HEADS_B))).astype(BF16)
        pad8 = lambda vec: jnp.pad(vec, (0, LANES - N_HEADS_B)).reshape(1, LANES)
        wts = (wb16(o_q, o_k), wb16(o_k, o_v), wb16(o_v, o_za), wb16(o_za, o_qkvb), wb16(o_qkvb, o_zb),
               wb16(o_zb, o_ab), w_abp, wb16(o_g, o_g + 2 * d), gmat,
               jnp.tile(q_norm_w[li], 2 * N_HEADS_A).reshape(1, A_W),
               jnp.tile(k_norm_w[li], 2 * N_HEADS_A).reshape(1, A_W),
               pad8(a_log[li]), pad8(dt_bias[li]))
        wa16 = w_branch_a[li].astype(BF16)
        wbb16 = w_branch_b[li].astype(BF16)
        wo16 = w_out[li].astype(BF16)
        lamp = jnp.stack([lam_q1[li], lam_k1[li], lam_q2[li], lam_k2[li]])

        n_c = bsz + nb
        c_all = jnp.pad(jnp.concatenate([c_prompt, c_sample], axis=0), ((0, (-n_c) % 16), (0, 0)))
        mod = _ada(c_all, w_ada[li], b_ada[li])
        shift, scale, gate = mod[:, :d], mod[:, d:2 * d], mod[:, 2 * d:]

        tm = _tile(s_len, 1024)
        h = _hnorm(yp, shift[:bsz, None], scale[:bsz, None], norm_w[li], tm)
        q16, kt32, kt16, v32, v16, za, raw, zb, gb, gates = _project(h.reshape(m_p, d), wts, tm, (bsz, s_len))
        oa = _attn_prompt(q16, kt16, v16, lamp, subln_w[li], bsz, s_len, _tile(s_len, 512), lambda_init)
        qkv = _conv_prompt(raw, conv_w[li], bsz, s_len, _tile(s_len, 512))
        ob, ssm_p = _delta_prompt(qkv, gb, zb, o_norm_w[li], bsz, s_len)
        yp = _out_proj(oa, za, ob, gates, yp, gate[:bsz, None], wa16, wbb16, wo16, _tile(s_len, 256))
        outs[0].append(jnp.transpose(kt32.reshape(bsz, N_HEADS_A, 2, HEAD_DIM_A, s_len), (0, 4, 1, 2, 3)))
        outs[1].append(v32.reshape(bsz, s_len, N_HEADS_A, 2 * HEAD_DIM_A))
        outs[4].append(ssm_p)
        outs[6].append(raw.reshape(bsz, s_len, cdim)[:, s_len - (CONV_W - 1):])

        xs = ys.reshape(1, nb, d)
        sl_s = slice(bsz, bsz + nb)
        h = _hnorm(xs, shift[None, sl_s], scale[None, sl_s], norm_w[li], nb)
        q16, k32, k16, v32, v16, za, raw, zb, gb, gates = _project(h.reshape(nb, d), wts, nb)
        n_pool = cache_k.shape[1]
        cache_kt = jnp.transpose(cache_k[li], (0, 2, 3, 4, 1)).reshape(n_pool, A_W, PAGE_SIZE)
        oa = _attn_sample(q16.astype(F32).reshape(nb, 1, A_W), k32.reshape(nb, 1, A_W), v32.reshape(nb, 1, A_W),
                          cache_kt, cache_v[li], page_table, lamp, subln_w[li], lambda_init)
        ob, ssm_s = _delta_sample(raw.reshape(nb, 1, cdim), state_conv[li], conv_w[li], gb.reshape(nb, 1, LANES),
                                  zb.reshape(nb, 1, B_W), o_norm_w[li], state_ssm[li])
        ys = _out_proj(oa.reshape(nb, A_W), za, ob.reshape(nb, B_W), gates, xs, gate[None, sl_s],
                       wa16, wbb16, wo16, nb).reshape(nb, 1, d)
        outs[2].append(k32.reshape(nb, 1, N_HEADS_A, 2, HEAD_DIM_A))
        outs[3].append(v32.reshape(nb, 1, N_HEADS_A, 2 * HEAD_DIM_A))
        outs[5].append(ssm_s)
        outs[7].append(jnp.concatenate([state_conv[li][:, 1:], raw.reshape(nb, 1, cdim)], axis=1))

    st = [jnp.stack(o) for o in outs]
    return (yp, ys, st[0], st[1], st[2], st[3], st[4], st[5], st[6], st[7])
```

```python
import functools
import math

import jax
import jax.numpy as jnp
from jax import lax
from jax.experimental import pallas as pl
from jax.experimental.pallas import tpu as pltpu

F32 = jnp.float32
BF16 = jnp.bfloat16

N_HEADS_A = 8
HEAD_DIM_A = 64
A_W = N_HEADS_A * 2 * HEAD_DIM_A
N_HEADS_B = 8
DK_B = 128
DV_B = 128
B_W = N_HEADS_B * DK_B
CONV_W = 4
CHUNK = 64
PAGE_SIZE = 128
EPS = 1e-6
LANES = 128
SUBLANES = 8
VMEM_LIMIT = 56 * 1024 * 1024
PAGES_PER_STEP = 16

_NT = (((1,), (1,)), ((), ()))
_TN = (((0,), (0,)), ((), ()))


def _cparams(sem):
    return pltpu.CompilerParams(dimension_semantics=sem, vmem_limit_bytes=VMEM_LIMIT)


def _dot(a, b):
    return jnp.dot(a, b, preferred_element_type=F32)


def _dot_nt(a, b):
    return lax.dot_general(a, b, _NT, preferred_element_type=F32)


def _dot_tn(a, b):
    return lax.dot_general(a, b, _TN, preferred_element_type=F32)


def _sigmoid(x):
    return 1.0 / (1.0 + jnp.exp(-x))


def _silu(x):
    return x * _sigmoid(x)


def _ada_kernel(c_ref, w_ref, b_ref, o_ref):
    sc = _silu(c_ref[...]).astype(BF16)
    o_ref[...] = _dot(sc, w_ref[...].astype(BF16)) + b_ref[...]


def _ada(c, w_ada, b_ada):
    m, d = c.shape
    n = w_ada.shape[1]
    tn = 512
    return pl.pallas_call(
        _ada_kernel,
        grid=(n // tn,),
        in_specs=[pl.BlockSpec((m, d), lambda j: (0, 0)),
                  pl.BlockSpec((d, tn), lambda j: (0, j)),
                  pl.BlockSpec((1, tn), lambda j: (0, j))],
        out_specs=pl.BlockSpec((m, tn), lambda j: (0, j)),
        out_shape=jax.ShapeDtypeStruct((m, n), F32),
        compiler_params=_cparams(("parallel",)),
        name="ada",
    )(c, w_ada, b_ada.reshape(1, n))


def _hnorm_kernel(x_ref, shift_ref, scale_ref, nw_ref, h_ref):
    x = x_ref[0]
    ms = jnp.mean(x * x, axis=-1, keepdims=True)
    y = x * lax.rsqrt(ms + EPS) * nw_ref[...]
    h_ref[0] = (y * (1.0 + scale_ref[0]) + shift_ref[0]).astype(BF16)


def _hnorm(x, shift, scale, norm_w, tm):
    b, s, d = x.shape
    ts = tm if shift.shape[1] == s else 1
    mod_map = (lambda i, j: (i, j, 0)) if ts == tm else (lambda i, j: (i, 0, 0))
    return pl.pallas_call(
        _hnorm_kernel,
        grid=(b, s // tm),
        in_specs=[pl.BlockSpec((1, tm, d), lambda i, j: (i, j, 0)),
                  pl.BlockSpec((1, ts, d), mod_map),
                  pl.BlockSpec((1, ts, d), mod_map),
                  pl.BlockSpec((1, d), lambda i, j: (0, 0))],
        out_specs=pl.BlockSpec((1, tm, d), lambda i, j: (i, j, 0)),
        out_shape=jax.ShapeDtypeStruct((b, s, d), BF16),
        compiler_params=_cparams(("parallel", "parallel")),
        name="hnorm",
    )(x, shift, scale, norm_w.reshape(1, d))


def _group_rms(acc, g_ref):
    sq = acc * acc
    hi = sq.astype(BF16)
    lo = (sq - hi.astype(F32)).astype(BF16)
    g = g_ref[...]
    outs = []
    for hh in range(acc.shape[1] // LANES):
        sl = slice(hh * LANES, (hh + 1) * LANES)
        ms = _dot(hi[:, sl], g) + _dot(lo[:, sl], g)
        outs.append(acc[:, sl] * lax.rsqrt(ms + EPS))
    return outs


def _proj_kernel(kind, h_ref, w_ref, *refs):
    acc = _dot(h_ref[...], w_ref[...])
    if kind == "q":
        g_ref, nw_ref, o_ref = refs
        for hh, y in enumerate(_group_rms(acc, g_ref)):
            sl = slice(hh * LANES, (hh + 1) * LANES)
            o_ref[:, sl] = (y * nw_ref[:, sl] * (HEAD_DIM_A ** -0.5)).astype(BF16)
    elif kind == "k":
        g_ref, nw_ref, o_ref, ob_ref = refs
        for hh, y in enumerate(_group_rms(acc, g_ref)):
            sl = slice(hh * LANES, (hh + 1) * LANES)
            y = y * nw_ref[:, sl]
            o_ref[:, sl] = y
            ob_ref[:, sl] = y.astype(BF16)
    elif kind == "v":
        o_ref, ob_ref = refs
        o_ref[...] = acc
        ob_ref[...] = acc.astype(BF16)
    elif kind == "silu":
        (o_ref,) = refs
        o_ref[...] = _silu(acc)
    elif kind == "sigmoid":
        (o_ref,) = refs
        o_ref[...] = _sigmoid(acc)
    elif kind == "raw":
        (o_ref,) = refs
        o_ref[...] = acc
    elif kind == "ab":
        alog_ref, dtb_ref, o_ref = refs
        a = acc + dtb_ref[...]
        sp = jnp.maximum(a, 0.0) + jnp.log(1.0 + jnp.exp(-jnp.abs(a)))
        g = -jnp.exp(alog_ref[...]) * sp
        lane = lax.broadcasted_iota(jnp.int32, acc.shape, 1)
        o_ref[...] = jnp.where(lane < N_HEADS_B, g, _sigmoid(acc))
    else:
        raise ValueError(kind)


def _proj_kt_kernel(h_ref, wt_ref, nw_ref, o_ref, ob_ref):
    acc = _dot_nt(wt_ref[...], h_ref[...])
    tm = acc.shape[1]
    x3 = acc.reshape(2 * N_HEADS_A, HEAD_DIM_A, tm)
    ms = jnp.mean(x3 * x3, axis=1, keepdims=True)
    y = (x3 * lax.rsqrt(ms + EPS) * nw_ref[...]).reshape(A_W, tm)
    o_ref[0] = y
    ob_ref[0] = y.astype(BF16)


def _proj_kt(h, wt, nw_col, bsz, s_len, tm):
    k = h.shape[1]
    nt = s_len // tm
    out_spec = pl.BlockSpec((1, A_W, tm), lambda i: (i // nt, 0, i % nt))
    return pl.pallas_call(
        _proj_kt_kernel,
        grid=(bsz * nt,),
        in_specs=[pl.BlockSpec((tm, k), lambda i: (i, 0)),
                  pl.BlockSpec((A_W, k), lambda i: (0, 0)),
                  pl.BlockSpec(nw_col.shape, lambda i: (0, 0, 0))],
        out_specs=[out_spec, out_spec],
        out_shape=[jax.ShapeDtypeStruct((bsz, A_W, s_len), F32),
                   jax.ShapeDtypeStruct((bsz, A_W, s_len), BF16)],
        compiler_params=_cparams(("parallel",)),
        name="proj_kt",
    )(h, wt, nw_col)


def _proj(kind, h, w, extras, out_dtypes, tm, tn):
    m, k = h.shape
    n = w.shape[1]
    extra_specs = [pl.BlockSpec(e.shape, lambda j, i, nd=e.ndim: (0,) * nd) for e in extras]
    if kind in ("q", "k"):
        extra_specs[1] = pl.BlockSpec((1, tn), lambda j, i: (0, j))
    outs = pl.pallas_call(
        functools.partial(_proj_kernel, kind),
        grid=(n // tn, m // tm),
        in_specs=[pl.BlockSpec((tm, k), lambda j, i: (i, 0)),
                  pl.BlockSpec((k, tn), lambda j, i: (0, j))] + extra_specs,
        out_specs=[pl.BlockSpec((tm, tn), lambda j, i: (i, j)) for _ in out_dtypes],
        out_shape=[jax.ShapeDtypeStruct((m, n), dt) for dt in out_dtypes],
        compiler_params=_cparams(("parallel", "parallel")),
        name="proj_" + kind,
    )(h, w, *extras)
    return outs


def _lam(lamp_ref, lambda_init):
    lp = lamp_ref[...]
    return (jnp.exp(jnp.sum(lp[0:1] * lp[1:2], axis=-1, keepdims=True))
            - jnp.exp(jnp.sum(lp[2:3] * lp[3:4], axis=-1, keepdims=True)) + lambda_init)


def _subln(o, sw, lambda_init):
    ms = jnp.mean(o * o, axis=-1, keepdims=True)
    return o * lax.rsqrt(ms + EPS) * sw * (1.0 - lambda_init)


def _attn_kernel(lamp_ref, q_ref, k_ref, v_ref, sw_ref, o_ref, m_sc, l_sc, acc_sc, *, tq, lambda_init):
    i = pl.program_id(2)
    q = q_ref[...]
    lane = lax.broadcasted_iota(jnp.int32, q.shape, 1)
    zero = jnp.zeros_like(q)
    qs = (jnp.where(lane < HEAD_DIM_A, q, zero), jnp.where(lane >= HEAD_DIM_A, q, zero))
    m_sc[...] = jnp.full(m_sc.shape, -jnp.inf, F32)
    l_sc[...] = jnp.zeros(l_sc.shape, F32)
    acc_sc[...] = jnp.zeros(acc_sc.shape, F32)

    def step(j, masked):
        start = pl.multiple_of(j * tq, tq)
        kb = k_ref[0, :, pl.ds(start, tq)]
        vb = v_ref[pl.ds(start, tq), :]
        for c in range(2):
            s = _dot(qs[c], kb)
            if masked:
                row = lax.broadcasted_iota(jnp.int32, s.shape, 0)
                col = lax.broadcasted_iota(jnp.int32, s.shape, 1)
                s = jnp.where(col <= row, s, -jnp.inf)
            m_prev = m_sc[c]
            m_new = jnp.maximum(m_prev, jnp.max(s, axis=-1, keepdims=True))
            alpha = jnp.exp(m_prev - m_new)
            ps = [jnp.exp(s[:, g * LANES:(g + 1) * LANES] - m_new) for g in range(tq // LANES)]
            l_sc[c] = alpha * l_sc[c] + functools.reduce(lambda a, b: a + b, ps)
            p = jnp.concatenate([x.astype(BF16) for x in ps], axis=1)
            acc_sc[c] = alpha * acc_sc[c] + _dot(p, vb)
            m_sc[c] = m_new

    def body(j, carry):
        step(j, False)
        return carry

    lax.fori_loop(0, i, body, 0)
    step(i, True)
    lam = _lam(lamp_ref, lambda_init)
    l0 = jnp.sum(l_sc[0], axis=-1, keepdims=True)
    l1 = jnp.sum(l_sc[1], axis=-1, keepdims=True)
    o = acc_sc[0] / l0 - lam * (acc_sc[1] / l1)
    o_ref[...] = _subln(o, sw_ref[...], lambda_init)


def _attn_prompt(q, k, v, lamp, subln_w, bsz, s_len, tq, lambda_init):
    m = q.shape[0]
    nq = s_len // tq
    return pl.pallas_call(
        functools.partial(_attn_kernel, tq=tq, lambda_init=lambda_init),
        grid=(bsz, N_HEADS_A, nq),
        in_specs=[pl.BlockSpec(lamp.shape, lambda b, h, i: (0, 0)),
                  pl.BlockSpec((tq, LANES), lambda b, h, i: (b * nq + i, h)),
                  pl.BlockSpec((1, LANES, s_len), lambda b, h, i: (b, h, 0)),
                  pl.BlockSpec((s_len, LANES), lambda b, h, i: (b, h)),
                  pl.BlockSpec((1, LANES), lambda b, h, i: (0, 0))],
        out_specs=pl.BlockSpec((tq, LANES), lambda b, h, i: (b * nq + i, h)),
        out_shape=jax.ShapeDtypeStruct((m, A_W), F32),
        scratch_shapes=[pltpu.VMEM((2, tq, LANES), F32)] * 3,
        compiler_params=_cparams(("parallel", "parallel", "parallel")),
        name="attn_prompt",
    )(lamp, q, k, v, subln_w.reshape(1, LANES))


def _decode_stages(qbd, own_head, e_ref, k_refs, v_refs, state, res):
    m_prev, l_prev, acc_prev = state
    parts = []
    pair = 2 if len(k_refs) % 2 == 0 else 1
    for p_i in range(0, len(k_refs), pair):
        kt = jnp.concatenate([k_refs[p_i + x][0].astype(BF16) for x in range(pair)], axis=1)
        parts.append(_dot(qbd, kt))
        yield
    s = jnp.concatenate(parts, axis=1)
    m_new = jnp.maximum(m_prev, jnp.max(s, axis=-1, keepdims=True))
    alpha = jnp.exp(m_prev - m_new)
    p = jnp.exp(s - m_new)
    res["m"] = m_new
    res["l"] = alpha * l_prev + jnp.sum(p, axis=-1, keepdims=True)
    p16 = p.astype(BF16)
    yield
    pv = jnp.zeros(acc_prev.shape, F32)
    for p_i, v_ref in enumerate(v_refs):
        pe = _dot(p16[:, p_i * PAGE_SIZE:(p_i + 1) * PAGE_SIZE], e_ref[...])
        pe = jnp.where(own_head, pe, 0.0).astype(BF16)
        v2 = v_ref[0].reshape(PAGE_SIZE * N_HEADS_A, LANES).astype(BF16)
        pv = pv + _dot(pe, v2)
        yield
    res["acc"] = alpha * acc_prev + pv


def _dec_attn_kernel(pt_ref, lamp_ref, q_ref, kn_ref, vn_ref, sw_ref, e_ref, *refs, n_pages_step, lambda_init,
                     delta):
    k_refs = refs[:n_pages_step]
    v_refs = refs[n_pages_step:2 * n_pages_step]
    rest = refs[2 * n_pages_step:]
    if delta is None:
        o_ref, m_sc, l_sc, acc_sc = rest
    else:
        dq_ref, dk_ref, dv_ref, gb_ref, zb_ref, onw_ref, o_ref, ob_ref, s_ref, m_sc, l_sc, acc_sc = rest
    j = pl.program_id(1)
    nrow = 2 * N_HEADS_A
    q = q_ref[0]
    hc = lax.broadcasted_iota(jnp.int32, (nrow, A_W), 0)
    lane = lax.broadcasted_iota(jnp.int32, (nrow, A_W), 1)
    qbd = jnp.where(lane // HEAD_DIM_A == hc, jnp.broadcast_to(q, (nrow, A_W)), 0.0).astype(BF16)
    own_head = lane % N_HEADS_A == hc // 2

    @pl.when(j == 0)
    def _():
        kn = jnp.broadcast_to(kn_ref[0], (nrow, A_W)).astype(BF16).astype(F32)
        m_sc[...] = jnp.sum(qbd.astype(F32) * kn, axis=-1, keepdims=True)
        l_sc[...] = jnp.ones(l_sc.shape, F32)
        vn = vn_ref[0].astype(BF16).astype(F32)
        row_head = lax.broadcasted_iota(jnp.int32, (nrow, LANES), 0) // 2
        acc = jnp.zeros((nrow, LANES), F32)
        for h in range(N_HEADS_A):
            acc = jnp.where(row_head == h, jnp.broadcast_to(vn[:, h * LANES:(h + 1) * LANES], (nrow, LANES)), acc)
        acc_sc[...] = acc

    res = {}
    gens = [_decode_stages(qbd, own_head, e_ref, k_refs, v_refs, (m_sc[...], l_sc[...], acc_sc[...]), res)]
    if delta is not None:
        hpu, ng, nc, n_units = delta
        t = pl.program_id(0) * pl.num_programs(1) + j
        unit = jnp.minimum(t, n_units - 1)
        h0 = pl.multiple_of((unit % ng) * hpu, hpu)

        @pl.when(((unit // ng) % nc == 0) & (t < n_units))
        def _():
            s_ref[0, pl.ds(h0, hpu)] = jnp.zeros((hpu, DK_B, DV_B), F32)

        dres = {}
        gens.append(_delta_stages(_head_slabs(dq_ref, hpu), _head_slabs(dk_ref, hpu), _head_slabs(dv_ref, hpu),
                                  gb_ref[0], s_ref[0, pl.ds(h0, hpu)], hpu, dres))
    _interleave(*gens)
    m_sc[...] = res["m"]
    l_sc[...] = res["l"]
    acc_sc[...] = res["acc"]
    if delta is not None:
        slabs = _delta_finish(dres["o"], zb_ref, onw_ref, hpu)

        @pl.when(t < n_units)
        def _():
            s_ref[0, pl.ds(h0, hpu)] = dres["s"]
            for h, slab in enumerate(slabs):
                ob_ref[:, h * LANES:(h + 1) * LANES] = slab

    @pl.when(j == pl.num_programs(1) - 1)
    def _():
        lam = _lam(lamp_ref, lambda_init)
        o = acc_sc[...] / l_sc[...]
        for h in range(N_HEADS_A):
            oh = o[2 * h:2 * h + 1] - lam * o[2 * h + 1:2 * h + 2]
            o_ref[0, :, h * LANES:(h + 1) * LANES] = _subln(oh, sw_ref[...], lambda_init)


def _delta_units(bsz, s_len, n_steps):
    nc = s_len // CHUNK
    for hpu in (1, 2, 4, 8):
        if bsz * nc * (N_HEADS_B // hpu) <= n_steps:
            return hpu
    return None


def _attn_sample(q, kn, vn, cache_kt, cache_v, page_table, lamp, subln_w, lambda_init, delta_args=None):
    nb, n_pages = page_table.shape
    pps = PAGES_PER_STEP if n_pages % PAGES_PER_STEP == 0 else 1
    nj = n_pages // pps
    row_spec = pl.BlockSpec((1, 1, A_W), lambda b, j, pt: (b, 0, 0))
    d_in_specs, d_out_specs, d_out_shapes, d_args, delta = [], [], [], [], None
    if delta_args is not None:
        qkv, gb, zb, o_norm_w, bsz, s_len, hpu = delta_args
        nc = s_len // CHUNK
        ng = N_HEADS_B // hpu
        n_units = bsz * nc * ng
        delta = (hpu, ng, nc, n_units)
        unit = lambda b, j: jnp.minimum(b * nj + j, n_units - 1)
        cols = lambda seg: pl.BlockSpec(
            (CHUNK, hpu * LANES), lambda b, j, pt, seg=seg: (unit(b, j) // ng, seg * ng + unit(b, j) % ng))
        d_in_specs = [cols(0), cols(1), cols(2),
                      pl.BlockSpec((1, CHUNK, LANES), lambda b, j, pt: (unit(b, j) % ng, unit(b, j) // ng, 0)),
                      cols(0),
                      pl.BlockSpec((1, LANES), lambda b, j, pt: (0, 0))]
        d_out_specs = [cols(0), pl.BlockSpec((1, N_HEADS_B, DK_B, DV_B),
                                             lambda b, j, pt: (unit(b, j) // (ng * nc), 0, 0, 0))]
        d_out_shapes = [jax.ShapeDtypeStruct((qkv.shape[0], B_W), BF16),
                        jax.ShapeDtypeStruct((bsz, N_HEADS_B, DK_B, DV_B), F32)]
        d_args = [qkv, qkv, qkv, _group_gates(gb, hpu), zb, o_norm_w.reshape(1, LANES)]

    def page_spec(p_i, shape):
        return pl.BlockSpec((1,) + shape, lambda b, j, pt: (pt[b, j * pps + p_i],) + (0,) * len(shape))

    expand = (jnp.arange(PAGE_SIZE)[:, None] == jnp.arange(PAGE_SIZE * N_HEADS_A)[None, :] // N_HEADS_A).astype(BF16)
    grid_spec = pltpu.PrefetchScalarGridSpec(
        num_scalar_prefetch=1,
        grid=(nb, nj),
        in_specs=[pl.BlockSpec(lamp.shape, lambda b, j, pt: (0, 0)), row_spec, row_spec, row_spec,
                  pl.BlockSpec((1, LANES), lambda b, j, pt: (0, 0)),
                  pl.BlockSpec(expand.shape, lambda b, j, pt: (0, 0))]
                 + [page_spec(p_i, (A_W, PAGE_SIZE)) for p_i in range(pps)]
                 + [page_spec(p_i, (PAGE_SIZE, N_HEADS_A, LANES)) for p_i in range(pps)]
                 + d_in_specs,
        out_specs=[row_spec] + d_out_specs,
        scratch_shapes=[pltpu.VMEM((2 * N_HEADS_A, 1), F32), pltpu.VMEM((2 * N_HEADS_A, 1), F32),
                        pltpu.VMEM((2 * N_HEADS_A, LANES), F32)],
    )
    outs = pl.pallas_call(
        functools.partial(_dec_attn_kernel, n_pages_step=pps, lambda_init=lambda_init, delta=delta),
        grid_spec=grid_spec,
        out_shape=[jax.ShapeDtypeStruct((nb, 1, A_W), F32)] + d_out_shapes,
        compiler_params=_cparams(("arbitrary", "arbitrary")),
        name="attn_sample",
    )(page_table, lamp, q, kn, vn, subln_w.reshape(1, LANES), expand,
      *([cache_kt] * pps), *([cache_v] * pps), *d_args)
    return outs if delta_args is not None else outs[0]


def _l2n(y):
    return y * lax.rsqrt(jnp.sum(y * y, axis=-1, keepdims=True) + EPS)


def _conv_kernel(x_ref, w_ref, o_ref, ext_ref, *, tm):
    seg = pl.program_id(1)
    t = pl.program_id(2)

    @pl.when(t == 0)
    def _():
        ext_ref[0:SUBLANES, :] = jnp.zeros((SUBLANES, ext_ref.shape[1]), F32)

    ext_ref[SUBLANES:SUBLANES + tm, :] = x_ref[...]
    w = w_ref[...]
    y = ext_ref[SUBLANES - 3:SUBLANES - 3 + tm, :] * w[0:1]
    for jj in range(1, CONV_W):
        off = SUBLANES - 3 + jj
        y = y + ext_ref[off:off + tm, :] * w[jj:jj + 1]
    y = _silu(y)
    qscale = jnp.where(seg == 0, DK_B ** -0.5, 1.0)
    for h in range(N_HEADS_B):
        sl = slice(h * LANES, (h + 1) * LANES)
        yh = y[:, sl]
        o_ref[:, sl] = jnp.where(seg < 2, _l2n(yh) * qscale, yh)
    ext_ref[0:SUBLANES, :] = ext_ref[tm:tm + SUBLANES, :]


def _conv_prompt(raw, conv_w, bsz, s_len, tm):
    m, cdim = raw.shape
    nt = s_len // tm
    return pl.pallas_call(
        functools.partial(_conv_kernel, tm=tm),
        grid=(bsz, cdim // B_W, nt),
        in_specs=[pl.BlockSpec((tm, B_W), lambda b, g, t: (b * nt + t, g)),
                  pl.BlockSpec((CONV_W, B_W), lambda b, g, t: (0, g))],
        out_specs=pl.BlockSpec((tm, B_W), lambda b, g, t: (b * nt + t, g)),
        out_shape=jax.ShapeDtypeStruct((m, cdim), F32),
        scratch_shapes=[pltpu.VMEM((tm + SUBLANES, B_W), F32)],
        compiler_params=_cparams(("parallel", "parallel", "arbitrary")),
        name="conv_prompt",
    )(raw, conv_w)


def _bdot(a, b):
    return lax.dot_general(a, b, (((2,), (1,)), ((0,), (0,))), preferred_element_type=F32)


def _bdot_nt(a, b):
    return lax.dot_general(a, b, (((2,), (2,)), ((0,), (0,))), preferred_element_type=F32)


def _bdot_tn(a, b):
    return lax.dot_general(a, b, (((1,), (1,)), ((0,), (0,))), preferred_element_type=F32)


def _split_bdot(a, b):
    ah = a.astype(BF16)
    al = (a - ah.astype(F32)).astype(BF16)
    bh = b.astype(BF16)
    bl = (b - bh.astype(F32)).astype(BF16)
    return _bdot(ah, bh) + _bdot(ah, bl) + _bdot(al, bh)


def _cumsum_rows(x):
    n = x.shape[0]
    row = lax.broadcasted_iota(jnp.int32, x.shape, 0)
    s = 1
    while s < n:
        x = x + jnp.where(row >= s, pltpu.roll(x, s, axis=0), 0.0)
        s *= 2
    return x


def _delta_stages(q, k, v, gb, s_old, nh, res):
    lane_bcast = lambda x, off: jnp.stack(
        [jnp.broadcast_to(x[:, off + h:off + h + 1], (CHUNK, LANES)) for h in range(nh)])
    gc_all = _cumsum_rows(gb)
    gc_t = gc_all.T
    gc = lane_bcast(gc_all, 0)
    gc_row = jnp.stack([gc_t[h:h + 1, :] for h in range(nh)])
    beta = lane_bcast(gb, nh)
    row = lax.broadcasted_iota(jnp.int32, (1, CHUNK, CHUNK), 1)
    col = lax.broadcasted_iota(jnp.int32, (1, CHUNK, CHUNK), 2)
    tril = col <= row
    strict = col < row
    eye = jnp.where(row == col, 1.0, 0.0)
    decay = jnp.exp(jnp.where(tril, gc[:, :, :CHUNK] - gc_row, -jnp.inf))
    kb = k * beta
    k16 = k.astype(BF16)
    lmat = jnp.where(strict, _bdot_nt(kb.astype(BF16), k16) * decay, 0.0)
    qk = jnp.where(tril, _bdot_nt(q.astype(BF16), k16) * decay, 0.0)
    yield
    tinv = eye - lmat
    pw = lmat
    for _ in range(5):
        pw = _split_bdot(pw, pw)
        yield
        tinv = tinv + _split_bdot(tinv, pw)
        yield
    t16 = tinv.astype(BF16)
    u = _bdot(t16, (v * beta).astype(BF16))
    wk = _bdot(t16, (kb * jnp.exp(gc)).astype(BF16))
    s16 = s_old.astype(BF16)
    yield
    v_new = u - _bdot(wk.astype(BF16), s16)
    vn16 = v_new.astype(BF16)
    yield
    res["o"] = _bdot((q * jnp.exp(gc)).astype(BF16), s16) + _bdot(qk.astype(BF16), vn16)
    g_last = gc[:, CHUNK - 1:CHUNK, :]
    k_dec = k * jnp.exp(g_last - gc)
    res["s"] = s_old * jnp.exp(g_last) + _bdot_tn(k_dec.astype(BF16), vn16)


def _interleave(*gens):
    live = list(gens)
    while live:
        for g in list(live):
            try:
                next(g)
            except StopIteration:
                live.remove(g)


def _head_slabs(ref, nh):
    return jnp.stack([ref[:, h * LANES:(h + 1) * LANES] for h in range(nh)])


def _delta_finish(o, zb_ref, onw_ref, nh):
    on = o * lax.rsqrt(jnp.mean(o * o, axis=-1, keepdims=True) + EPS) * onw_ref[...]
    return [(on[h] * zb_ref[:, h * LANES:(h + 1) * LANES]).astype(BF16) for h in range(nh)]


def _delta_kernel(q_ref, k_ref, v_ref, gb_ref, zb_ref, onw_ref, ob_ref, s_ref):
    c = pl.program_id(1)
    nh = N_HEADS_B

    @pl.when(c == 0)
    def _():
        s_ref[...] = jnp.zeros(s_ref.shape, F32)

    res = {}
    _interleave(_delta_stages(_head_slabs(q_ref, nh), _head_slabs(k_ref, nh), _head_slabs(v_ref, nh),
                              gb_ref[0], s_ref[0], nh, res))
    s_ref[0] = res["s"]
    for h, slab in enumerate(_delta_finish(res["o"], zb_ref, onw_ref, nh)):
        ob_ref[:, h * LANES:(h + 1) * LANES] = slab


def _group_gates(gb, hpu):
    m = gb.shape[0]
    ng = N_HEADS_B // hpu
    g = gb[:, :N_HEADS_B].reshape(m, ng, hpu)
    beta = gb[:, N_HEADS_B:2 * N_HEADS_B].reshape(m, ng, hpu)
    gbg = jnp.transpose(jnp.concatenate([g, beta], axis=-1), (1, 0, 2))
    return jnp.pad(gbg, ((0, 0), (0, 0), (0, LANES - 2 * hpu)))


def _delta_prompt(qkv, gb, zb, o_norm_w, bsz, s_len):
    m = qkv.shape[0]
    nc = s_len // CHUNK
    seg = lambda g: pl.BlockSpec((CHUNK, B_W), lambda b, c, g=g: (b * nc + c, g))
    return pl.pallas_call(
        _delta_kernel,
        grid=(bsz, nc),
        in_specs=[seg(0), seg(1), seg(2),
                  pl.BlockSpec((1, CHUNK, LANES), lambda b, c: (0, b * nc + c, 0)),
                  seg(0),
                  pl.BlockSpec((1, LANES), lambda b, c: (0, 0))],
        out_specs=[seg(0), pl.BlockSpec((1, N_HEADS_B, DK_B, DV_B), lambda b, c: (b, 0, 0, 0))],
        out_shape=[jax.ShapeDtypeStruct((m, B_W), BF16),
                   jax.ShapeDtypeStruct((bsz, N_HEADS_B, DK_B, DV_B), F32)],
        compiler_params=_cparams(("parallel", "arbitrary")),
        name="delta_prompt",
    )(qkv, qkv, qkv, _group_gates(gb, N_HEADS_B), zb, o_norm_w.reshape(1, LANES))


def _dec_delta_kernel(u_ref, buf_ref, w_ref, gb_ref, zb_ref, onw_ref, s0_ref, ob_ref, s_ref):
    w = w_ref[...]
    buf = buf_ref[0]
    y = buf[0:1] * w[0:1] + buf[1:2] * w[1:2] + buf[2:3] * w[2:3] + u_ref[0] * w[3:4]
    y = _silu(y)
    gb = gb_ref[0]
    r = lax.broadcasted_iota(jnp.int32, (DK_B, LANES), 0)
    cidx = lax.broadcasted_iota(jnp.int32, (DK_B, LANES), 1)
    eye = r == cidx

    def column(x):
        return jnp.sum(jnp.where(eye, jnp.broadcast_to(x, (DK_B, LANES)), 0.0), axis=-1, keepdims=True)

    for h in range(N_HEADS_B):
        sl = slice(h * LANES, (h + 1) * LANES)
        q = _l2n(y[:, h * LANES:(h + 1) * LANES]) * (DK_B ** -0.5)
        k = _l2n(y[:, B_W + h * LANES:B_W + (h + 1) * LANES])
        v = y[:, 2 * B_W + h * LANES:2 * B_W + (h + 1) * LANES]
        a = jnp.exp(gb[:, h:h + 1])
        beta = gb[:, N_HEADS_B + h:N_HEADS_B + h + 1]
        s_old = s0_ref[0, h]
        k_col = column(k)
        q_col = column(q)
        v_new = beta * (v - a * jnp.sum(k_col * s_old, axis=0, keepdims=True))
        s_new = a * s_old + k_col * v_new
        s_ref[0, h] = s_new
        o = jnp.sum(q_col * s_new, axis=0, keepdims=True)
        on = o * lax.rsqrt(jnp.mean(o * o, axis=-1, keepdims=True) + EPS) * onw_ref[...]
        ob_ref[0, :, sl] = (on * zb_ref[0, :, sl]).astype(BF16)


def _delta_sample(u, conv_buf, conv_w, gb, zb, o_norm_w, s0):
    nb, _, cdim = u.shape
    row = lambda wd: pl.BlockSpec((1, 1, wd), lambda b: (b, 0, 0))
    st = pl.BlockSpec((1, N_HEADS_B, DK_B, DV_B), lambda b: (b, 0, 0, 0))
    return pl.pallas_call(
        _dec_delta_kernel,
        grid=(nb,),
        in_specs=[row(cdim), pl.BlockSpec((1, CONV_W - 1, cdim), lambda b: (b, 0, 0)),
                  pl.BlockSpec((CONV_W, cdim), lambda b: (0, 0)), row(LANES), row(B_W),
                  pl.BlockSpec((1, LANES), lambda b: (0, 0)), st],
        out_specs=[row(B_W), st],
        out_shape=[jax.ShapeDtypeStruct((nb, 1, B_W), BF16),
                   jax.ShapeDtypeStruct(s0.shape, F32)],
        compiler_params=_cparams(("parallel",)),
        name="delta_sample",
    )(u, conv_buf, conv_w, gb, zb, o_norm_w.reshape(1, LANES), s0)


def _out_kernel(oa_ref, za_ref, ob_ref, ga_ref, gbt_ref, x_ref, gate_ref, wa_ref, wb_ref, wo_ref, y_ref):
    ya = _dot((oa_ref[...] * za_ref[...]).astype(BF16), wa_ref[...])
    yb = _dot(ob_ref[...], wb_ref[...])
    mrg = ga_ref[...] * ya + gbt_ref[...] * yb
    y_ref[0] = x_ref[0] + gate_ref[0] * _dot(mrg.astype(BF16), wo_ref[...])


def _out_proj(oa, za, ob, gates, x, gate, wa, wb, wo, tm):
    b, s, d = x.shape
    nt = s // tm
    ts = tm if gate.shape[1] == s else 1
    gate_map = (lambda i, j: (i, j, 0)) if ts == tm else (lambda i, j: (i, 0, 0))
    rows = lambda wd, cb=0: pl.BlockSpec((tm, wd), lambda i, j, cb=cb: (i * nt + j, cb))
    const = lambda shp: pl.BlockSpec(shp, lambda i, j: (0, 0), pipeline_mode=pl.Buffered(1))
    return pl.pallas_call(
        _out_kernel,
        grid=(b, nt),
        in_specs=[rows(A_W), rows(A_W), rows(B_W), rows(d, 0), rows(d, 1),
                  pl.BlockSpec((1, tm, d), lambda i, j: (i, j, 0)),
                  pl.BlockSpec((1, ts, d), gate_map),
                  const(wa.shape), const(wb.shape), const(wo.shape)],
        out_specs=pl.BlockSpec((1, tm, d), lambda i, j: (i, j, 0)),
        out_shape=jax.ShapeDtypeStruct((b, s, d), F32),
        compiler_params=_cparams(("parallel", "parallel")),
        name="out_proj",
    )(oa, za, ob, gates, gates, x, gate, wa, wb, wo)


def _tile(n, pref):
    t = min(n, pref)
    while n % t:
        t //= 2
    return t


def _project(h2, wts, tm, kt_dims=None):
    (w_q, w_k, w_v, w_za, w_qkvb, w_zb, w_ab, w_g, gmat, qnw, knw, alog, dtb) = wts
    tn = A_W
    (q16,) = _proj("q", h2, w_q, [gmat, qnw], [BF16], tm, tn)
    if kt_dims is None:
        k32, k16 = _proj("k", h2, w_k, [gmat, knw], [F32, BF16], tm, tn)
    else:
        knw_col = knw[0, :HEAD_DIM_A].reshape(1, HEAD_DIM_A, 1)
        k32, k16 = _proj_kt(h2, w_k.T, knw_col, kt_dims[0], kt_dims[1], tm)
    v32, v16 = _proj("v", h2, w_v, [], [F32, BF16], tm, tn)
    (za,) = _proj("silu", h2, w_za, [], [F32], tm, tn)
    (raw,) = _proj("raw", h2, w_qkvb, [], [F32], tm, tn)
    (zb,) = _proj("silu", h2, w_zb, [], [F32], tm, tn)
    (gb,) = _proj("ab", h2, w_ab, [alog, dtb], [F32], tm, LANES)
    (gates,) = _proj("sigmoid", h2, w_g, [], [F32], tm, tn)
    return q16, k32, k16, v32, v16, za, raw, zb, gb, gates


def kernel(x_prompt, x_sample, c_prompt, c_sample, cache_k, cache_v, page_table, state_ssm, state_conv, w_ada, b_ada, norm_w, w_in, q_norm_w, k_norm_w, lam_q1, lam_k1, lam_q2, lam_k2, subln_w, conv_w, a_log, dt_bias, o_norm_w, w_branch_a, w_branch_b, w_out):
    depth = w_ada.shape[0]
    bsz, s_len, d = x_prompt.shape
    nb, t_dec, _ = x_sample.shape
    assert t_dec == 1 and s_len % CHUNK == 0 and s_len >= CONV_W - 1
    m_p = bsz * s_len
    cdim = conv_w.shape[-1]
    o_q, o_k, o_v, o_za = 0, A_W, 2 * A_W, 3 * A_W
    o_qkvb = 4 * A_W
    o_zb = o_qkvb + cdim
    o_ab = o_zb + B_W
    o_g = o_ab + 2 * N_HEADS_B

    gi = jnp.arange(LANES) // HEAD_DIM_A
    gmat = jnp.where(gi[:, None] == gi[None, :], 1.0 / HEAD_DIM_A, 0.0).astype(BF16)

    yp, ys = x_prompt, x_sample
    outs = [[] for _ in range(8)]
    for li in range(depth):
        lambda_init = 0.8 - 0.6 * math.exp(-0.3 * li)
        wl = w_in[li]
        wb16 = lambda a, b: wl[:, a:b].astype(BF16)
        w_abp = jnp.pad(wl[:, o_ab:o_g], ((0, 0), (0, LANES - 2 * N_HEADS_B))).astype(BF16)
        pad8 = lambda vec: jnp.pad(vec, (0, LANES - N_HEADS_B)).reshape(1, LANES)
        wts = (wb16(o_q, o_k), wb16(o_k, o_v), wb16(o_v, o_za), wb16(o_za, o_qkvb), wb16(o_qkvb, o_zb),
               wb16(o_zb, o_ab), w_abp, wb16(o_g, o_g + 2 * d), gmat,
               jnp.tile(q_norm_w[li], 2 * N_HEADS_A).reshape(1, A_W),
               jnp.tile(k_norm_w[li], 2 * N_HEADS_A).reshape(1, A_W),
               pad8(a_log[li]), pad8(dt_bias[li]))
        wa16 = w_branch_a[li].astype(BF16)
        wbb16 = w_branch_b[li].astype(BF16)
        wo16 = w_out[li].astype(BF16)
        lamp = jnp.stack([lam_q1[li], lam_k1[li], lam_q2[li], lam_k2[li]])

        n_c = bsz + nb
        c_all = jnp.pad(jnp.concatenate([c_prompt, c_sample], axis=0), ((0, (-n_c) % 16), (0, 0)))
        mod = _ada(c_all, w_ada[li], b_ada[li])
        shift, scale, gate = mod[:, :d], mod[:, d:2 * d], mod[:, 2 * d:]

        tm = _tile(s_len, 1024)
        h = _hnorm(yp, shift[:bsz, None], scale[:bsz, None], norm_w[li], tm)
        q16, kt32, kt16, v32, v16, za, raw, zb, gb, gates = _project(h.reshape(m_p, d), wts, tm, (bsz, s_len))
        xs = ys.reshape(1, nb, d)
        sl_s = slice(bsz, bsz + nb)
        h_s = _hnorm(xs, shift[None, sl_s], scale[None, sl_s], norm_w[li], nb)
        q16_s, k32_s, _, v32_s, _, za_s, raw_s, zb_s, gb_s, gates_s = _project(h_s.reshape(nb, d), wts, nb)

        oa = _attn_prompt(q16, kt16, v16, lamp, subln_w[li], bsz, s_len, _tile(s_len, 512), lambda_init)
        qkv = _conv_prompt(raw, conv_w[li], bsz, s_len, _tile(s_len, 512))
        n_pool = cache_k.shape[1]
        cache_kt = jnp.transpose(cache_k[li], (0, 2, 3, 4, 1)).reshape(n_pool, A_W, PAGE_SIZE)
        dec_args = (q16_s.astype(F32).reshape(nb, 1, A_W), k32_s.reshape(nb, 1, A_W), v32_s.reshape(nb, 1, A_W),
                    cache_kt, cache_v[li], page_table, lamp, subln_w[li], lambda_init)
        n_pages = page_table.shape[1]
        n_steps = nb * (n_pages // (PAGES_PER_STEP if n_pages % PAGES_PER_STEP == 0 else 1))
        hpu = _delta_units(bsz, s_len, n_steps)
        if hpu is None:
            oa_s = _attn_sample(*dec_args)
            ob, ssm_p = _delta_prompt(qkv, gb, zb, o_norm_w[li], bsz, s_len)
        else:
            oa_s, ob, ssm_p = _attn_sample(*dec_args, delta_args=(qkv, gb, zb, o_norm_w[li], bsz, s_len, hpu))
        ob_s, ssm_s = _delta_sample(raw_s.reshape(nb, 1, cdim), state_conv[li], conv_w[li],
                                    gb_s.reshape(nb, 1, LANES), zb_s.reshape(nb, 1, B_W), o_norm_w[li],
                                    state_ssm[li])

        yp = _out_proj(oa, za, ob, gates, yp, gate[:bsz, None], wa16, wbb16, wo16, _tile(s_len, 256))
        ys = _out_proj(oa_s.reshape(nb, A_W), za_s, ob_s.reshape(nb, B_W), gates_s, xs, gate[None, sl_s],
                       wa16, wbb16, wo16, nb).reshape(nb, 1, d)
        outs[0].append(jnp.transpose(kt32.reshape(bsz, N_HEADS_A, 2, HEAD_DIM_A, s_len), (0, 4, 1, 2, 3)))
        outs[1].append(v32.reshape(bsz, s_len, N_HEADS_A, 2 * HEAD_DIM_A))
        outs[2].append(k32_s.reshape(nb, 1, N_HEADS_A, 2, HEAD_DIM_A))
        outs[3].append(v32_s.reshape(nb, 1, N_HEADS_A, 2 * HEAD_DIM_A))
        outs[4].append(ssm_p)
        outs[5].append(ssm_s)
        outs[6].append(raw.reshape(bsz, s_len, cdim)[:, s_len - (CONV_W - 1):])
        outs[7].append(jnp.concatenate([state_conv[li][:, 1:], raw_s.reshape(nb, 1, cdim)], axis=1))

    st = [jnp.stack(o) for o in outs]
    return (yp, ys, st[0], st[1], st[2], st[3], st[4], st[5], st[6], st[7])
```

```python
import functools
import math

import jax
import jax.numpy as jnp
from jax import lax
from jax.experimental import pallas as pl
from jax.experimental.pallas import tpu as pltpu

F32 = jnp.float32
BF16 = jnp.bfloat16

N_HEADS_A = 8
HEAD_DIM_A = 64
A_W = N_HEADS_A * 2 * HEAD_DIM_A
N_HEADS_B = 8
DK_B = 128
DV_B = 128
B_W = N_HEADS_B * DK_B
CONV_W = 4
CHUNK = 64
PAGE_SIZE = 128
EPS = 1e-6
LANES = 128
SUBLANES = 8
VMEM_LIMIT = 56 * 1024 * 1024
PAGES_PER_STEP = 16

_NT = (((1,), (1,)), ((), ()))
_TN = (((0,), (0,)), ((), ()))


def _cparams(sem):
    return pltpu.CompilerParams(dimension_semantics=sem, vmem_limit_bytes=VMEM_LIMIT)


def _dot(a, b):
    return jnp.dot(a, b, preferred_element_type=F32)


def _dot_nt(a, b):
    return lax.dot_general(a, b, _NT, preferred_element_type=F32)


def _dot_tn(a, b):
    return lax.dot_general(a, b, _TN, preferred_element_type=F32)


def _sigmoid(x):
    return 1.0 / (1.0 + jnp.exp(-x))


def _silu(x):
    return x * _sigmoid(x)


def _ada_kernel(c_ref, w_ref, b_ref, o_ref):
    sc = _silu(c_ref[...]).astype(BF16)
    o_ref[...] = _dot(sc, w_ref[...].astype(BF16)) + b_ref[...]


def _ada(c, w_ada, b_ada):
    m, d = c.shape
    n = w_ada.shape[1]
    tn = 512
    return pl.pallas_call(
        _ada_kernel,
        grid=(n // tn,),
        in_specs=[pl.BlockSpec((m, d), lambda j: (0, 0)),
                  pl.BlockSpec((d, tn), lambda j: (0, j)),
                  pl.BlockSpec((1, tn), lambda j: (0, j))],
        out_specs=pl.BlockSpec((m, tn), lambda j: (0, j)),
        out_shape=jax.ShapeDtypeStruct((m, n), F32),
        compiler_params=_cparams(("parallel",)),
        name="ada",
    )(c, w_ada, b_ada.reshape(1, n))


def _hnorm_kernel(x_ref, shift_ref, scale_ref, nw_ref, h_ref):
    x = x_ref[0]
    ms = jnp.mean(x * x, axis=-1, keepdims=True)
    y = x * lax.rsqrt(ms + EPS) * nw_ref[...]
    h_ref[0] = (y * (1.0 + scale_ref[0]) + shift_ref[0]).astype(BF16)


def _hnorm(x, shift, scale, norm_w, tm):
    b, s, d = x.shape
    ts = tm if shift.shape[1] == s else 1
    mod_map = (lambda i, j: (i, j, 0)) if ts == tm else (lambda i, j: (i, 0, 0))
    return pl.pallas_call(
        _hnorm_kernel,
        grid=(b, s // tm),
        in_specs=[pl.BlockSpec((1, tm, d), lambda i, j: (i, j, 0)),
                  pl.BlockSpec((1, ts, d), mod_map),
                  pl.BlockSpec((1, ts, d), mod_map),
                  pl.BlockSpec((1, d), lambda i, j: (0, 0))],
        out_specs=pl.BlockSpec((1, tm, d), lambda i, j: (i, j, 0)),
        out_shape=jax.ShapeDtypeStruct((b, s, d), BF16),
        compiler_params=_cparams(("parallel", "parallel")),
        name="hnorm",
    )(x, shift, scale, norm_w.reshape(1, d))


def _group_rms(acc, g_ref):
    sq = acc * acc
    hi = sq.astype(BF16)
    lo = (sq - hi.astype(F32)).astype(BF16)
    g = g_ref[...]
    outs = []
    for hh in range(acc.shape[1] // LANES):
        sl = slice(hh * LANES, (hh + 1) * LANES)
        ms = _dot(hi[:, sl], g) + _dot(lo[:, sl], g)
        outs.append(acc[:, sl] * lax.rsqrt(ms + EPS))
    return outs


def _proj_kernel(kind, h_ref, w_ref, *refs):
    acc = _dot(h_ref[...], w_ref[...])
    if kind == "q":
        g_ref, nw_ref, o_ref = refs
        for hh, y in enumerate(_group_rms(acc, g_ref)):
            sl = slice(hh * LANES, (hh + 1) * LANES)
            o_ref[:, sl] = (y * nw_ref[:, sl] * (HEAD_DIM_A ** -0.5)).astype(BF16)
    elif kind == "k":
        g_ref, nw_ref, o_ref, ob_ref = refs
        for hh, y in enumerate(_group_rms(acc, g_ref)):
            sl = slice(hh * LANES, (hh + 1) * LANES)
            y = y * nw_ref[:, sl]
            o_ref[:, sl] = y
            ob_ref[:, sl] = y.astype(BF16)
    elif kind == "v":
        o_ref, ob_ref = refs
        o_ref[...] = acc
        ob_ref[...] = acc.astype(BF16)
    elif kind == "silu":
        (o_ref,) = refs
        o_ref[...] = _silu(acc)
    elif kind == "sigmoid":
        (o_ref,) = refs
        o_ref[...] = _sigmoid(acc)
    elif kind == "raw":
        (o_ref,) = refs
        o_ref[...] = acc
    elif kind == "ab":
        alog_ref, dtb_ref, o_ref = refs
        a = acc + dtb_ref[...]
        sp = jnp.maximum(a, 0.0) + jnp.log(1.0 + jnp.exp(-jnp.abs(a)))
        g = -jnp.exp(alog_ref[...]) * sp
        lane = lax.broadcasted_iota(jnp.int32, acc.shape, 1)
        o_ref[...] = jnp.where(lane < N_HEADS_B, g, _sigmoid(acc))
    else:
        raise ValueError(kind)


def _proj_kt_kernel(h_ref, wt_ref, nw_ref, o_ref, ob_ref):
    acc = _dot_nt(wt_ref[...], h_ref[...])
    tm = acc.shape[1]
    x3 = acc.reshape(2 * N_HEADS_A, HEAD_DIM_A, tm)
    ms = jnp.mean(x3 * x3, axis=1, keepdims=True)
    y = (x3 * lax.rsqrt(ms + EPS) * nw_ref[...]).reshape(A_W, tm)
    o_ref[0] = y
    ob_ref[0] = y.astype(BF16)


def _proj_kt(h, wt, nw_col, bsz, s_len, tm):
    k = h.shape[1]
    nt = s_len // tm
    out_spec = pl.BlockSpec((1, A_W, tm), lambda i: (i // nt, 0, i % nt))
    return pl.pallas_call(
        _proj_kt_kernel,
        grid=(bsz * nt,),
        in_specs=[pl.BlockSpec((tm, k), lambda i: (i, 0)),
                  pl.BlockSpec((A_W, k), lambda i: (0, 0)),
                  pl.BlockSpec(nw_col.shape, lambda i: (0, 0, 0))],
        out_specs=[out_spec, out_spec],
        out_shape=[jax.ShapeDtypeStruct((bsz, A_W, s_len), F32),
                   jax.ShapeDtypeStruct((bsz, A_W, s_len), BF16)],
        compiler_params=_cparams(("parallel",)),
        name="proj_kt",
    )(h, wt, nw_col)


def _proj(kind, h, wseg, extras, out_dtypes, tm, tn):
    w, col0, n = wseg
    m, k = h.shape
    extra_specs = [pl.BlockSpec(e.shape, lambda j, i, nd=e.ndim: (0,) * nd) for e in extras]
    if kind in ("q", "k"):
        extra_specs[1] = pl.BlockSpec((1, tn), lambda j, i: (0, j))
    outs = pl.pallas_call(
        functools.partial(_proj_kernel, kind),
        grid=(n // tn, m // tm),
        in_specs=[pl.BlockSpec((tm, k), lambda j, i: (i, 0)),
                  pl.BlockSpec((k, tn), lambda j, i: (0, col0 + j))] + extra_specs,
        out_specs=[pl.BlockSpec((tm, tn), lambda j, i: (i, j)) for _ in out_dtypes],
        out_shape=[jax.ShapeDtypeStruct((m, n), dt) for dt in out_dtypes],
        compiler_params=_cparams(("parallel", "parallel")),
        name="proj_" + kind,
    )(h, w, *extras)
    return outs


def _lam(lamp_ref, lambda_init):
    lp = lamp_ref[...]
    return (jnp.exp(jnp.sum(lp[0:1] * lp[1:2], axis=-1, keepdims=True))
            - jnp.exp(jnp.sum(lp[2:3] * lp[3:4], axis=-1, keepdims=True)) + lambda_init)


def _subln(o, sw, lambda_init):
    ms = jnp.mean(o * o, axis=-1, keepdims=True)
    return o * lax.rsqrt(ms + EPS) * sw * (1.0 - lambda_init)


def _attn_kernel(lamp_ref, q_ref, k_ref, v_ref, sw_ref, o_ref, m_sc, l_sc, acc_sc, *, tq, lambda_init):
    i = pl.program_id(2)
    q = q_ref[...]
    lane = lax.broadcasted_iota(jnp.int32, q.shape, 1)
    zero = jnp.zeros_like(q)
    qs = (jnp.where(lane < HEAD_DIM_A, q, zero), jnp.where(lane >= HEAD_DIM_A, q, zero))
    m_sc[...] = jnp.full(m_sc.shape, -jnp.inf, F32)
    l_sc[...] = jnp.zeros(l_sc.shape, F32)
    acc_sc[...] = jnp.zeros(acc_sc.shape, F32)

    def step(j, masked):
        start = pl.multiple_of(j * tq, tq)
        kb = k_ref[0, :, pl.ds(start, tq)]
        vb = v_ref[pl.ds(start, tq), :]
        for c in range(2):
            s = _dot(qs[c], kb)
            if masked:
                row = lax.broadcasted_iota(jnp.int32, s.shape, 0)
                col = lax.broadcasted_iota(jnp.int32, s.shape, 1)
                s = jnp.where(col <= row, s, -jnp.inf)
            m_prev = m_sc[c]
            m_new = jnp.maximum(m_prev, jnp.max(s, axis=-1, keepdims=True))
            alpha = jnp.exp(m_prev - m_new)
            ps = [jnp.exp(s[:, g * LANES:(g + 1) * LANES] - m_new) for g in range(tq // LANES)]
            l_sc[c] = alpha * l_sc[c] + functools.reduce(lambda a, b: a + b, ps)
            p = jnp.concatenate([x.astype(BF16) for x in ps], axis=1)
            acc_sc[c] = alpha * acc_sc[c] + _dot(p, vb)
            m_sc[c] = m_new

    def body(j, carry):
        step(j, False)
        return carry

    lax.fori_loop(0, i, body, 0)
    step(i, True)
    lam = _lam(lamp_ref, lambda_init)
    l0 = jnp.sum(l_sc[0], axis=-1, keepdims=True)
    l1 = jnp.sum(l_sc[1], axis=-1, keepdims=True)
    o = acc_sc[0] / l0 - lam * (acc_sc[1] / l1)
    o_ref[...] = _subln(o, sw_ref[...], lambda_init)


def _attn_prompt(q, k, v, lamp, subln_w, bsz, s_len, tq, lambda_init):
    m = q.shape[0]
    nq = s_len // tq
    return pl.pallas_call(
        functools.partial(_attn_kernel, tq=tq, lambda_init=lambda_init),
        grid=(bsz, N_HEADS_A, nq),
        in_specs=[pl.BlockSpec(lamp.shape, lambda b, h, i: (0, 0)),
                  pl.BlockSpec((tq, LANES), lambda b, h, i: (b * nq + i, h)),
                  pl.BlockSpec((1, LANES, s_len), lambda b, h, i: (b, h, 0)),
                  pl.BlockSpec((s_len, LANES), lambda b, h, i: (b, h)),
                  pl.BlockSpec((1, LANES), lambda b, h, i: (0, 0))],
        out_specs=pl.BlockSpec((tq, LANES), lambda b, h, i: (b * nq + i, h)),
        out_shape=jax.ShapeDtypeStruct((m, A_W), F32),
        scratch_shapes=[pltpu.VMEM((2, tq, LANES), F32)] * 3,
        compiler_params=_cparams(("parallel", "parallel", "parallel")),
        name="attn_prompt",
    )(lamp, q, k, v, subln_w.reshape(1, LANES))


def _decode_stages(qbd, own_head, e_ref, k_refs, v_refs, state, res):
    m_prev, l_prev, acc_prev = state
    parts = []
    pair = 2 if len(k_refs) % 2 == 0 else 1
    for p_i in range(0, len(k_refs), pair):
        kt = jnp.concatenate([k_refs[p_i + x][0].astype(BF16) for x in range(pair)], axis=1)
        parts.append(_dot(qbd, kt))
        yield
    s = jnp.concatenate(parts, axis=1)
    m_new = jnp.maximum(m_prev, jnp.max(s, axis=-1, keepdims=True))
    alpha = jnp.exp(m_prev - m_new)
    p = jnp.exp(s - m_new)
    res["m"] = m_new
    res["l"] = alpha * l_prev + jnp.sum(p, axis=-1, keepdims=True)
    nrow = p.shape[0]
    p_rows = jnp.concatenate([p[:, i * PAGE_SIZE:(i + 1) * PAGE_SIZE] for i in range(len(v_refs))], axis=0)
    pe_all = _dot(p_rows.astype(BF16), e_ref[...])
    yield
    pv = jnp.zeros(acc_prev.shape, F32)
    for p_i, v_ref in enumerate(v_refs):
        pe = jnp.where(own_head, pe_all[p_i * nrow:(p_i + 1) * nrow], 0.0).astype(BF16)
        v2 = v_ref[0].reshape(PAGE_SIZE * N_HEADS_A, LANES).astype(BF16)
        pv = pv + _dot(pe, v2)
        yield
    res["acc"] = alpha * acc_prev + pv


def _dec_attn_kernel(pt_ref, lamp_ref, q_ref, kn_ref, vn_ref, sw_ref, e_ref, *refs, n_pages_step, lambda_init,
                     delta):
    k_refs = refs[:n_pages_step]
    v_refs = refs[n_pages_step:2 * n_pages_step]
    rest = refs[2 * n_pages_step:]
    if delta is None:
        o_ref, m_sc, l_sc, acc_sc = rest
    else:
        dq_ref, dk_ref, dv_ref, gb_ref, zb_ref, onw_ref, o_ref, ob_ref, s_ref, m_sc, l_sc, acc_sc = rest
    j = pl.program_id(1)
    nrow = 2 * N_HEADS_A
    q = q_ref[0]
    hc = lax.broadcasted_iota(jnp.int32, (nrow, A_W), 0)
    lane = lax.broadcasted_iota(jnp.int32, (nrow, A_W), 1)
    qbd = jnp.where(lane // HEAD_DIM_A == hc, jnp.broadcast_to(q, (nrow, A_W)), 0.0).astype(BF16)
    own_head = lane % N_HEADS_A == hc // 2

    @pl.when(j == 0)
    def _():
        kn = jnp.broadcast_to(kn_ref[0], (nrow, A_W)).astype(BF16).astype(F32)
        m_sc[...] = jnp.sum(qbd.astype(F32) * kn, axis=-1, keepdims=True)
        l_sc[...] = jnp.ones(l_sc.shape, F32)
        vn = vn_ref[0].astype(BF16).astype(F32)
        row_head = lax.broadcasted_iota(jnp.int32, (nrow, LANES), 0) // 2
        acc = jnp.zeros((nrow, LANES), F32)
        for h in range(N_HEADS_A):
            acc = jnp.where(row_head == h, jnp.broadcast_to(vn[:, h * LANES:(h + 1) * LANES], (nrow, LANES)), acc)
        acc_sc[...] = acc

    res = {}
    gens = [_decode_stages(qbd, own_head, e_ref, k_refs, v_refs, (m_sc[...], l_sc[...], acc_sc[...]), res)]
    if delta is not None:
        hpu, ng, nc, n_units = delta
        t = pl.program_id(0) * pl.num_programs(1) + j
        unit = jnp.minimum(t, n_units - 1)
        h0 = pl.multiple_of((unit % ng) * hpu, hpu)

        @pl.when(((unit // ng) % nc == 0) & (t < n_units))
        def _():
            s_ref[0, pl.ds(h0, hpu)] = jnp.zeros((hpu, DK_B, DV_B), F32)

        dres = {}
        gens.append(_delta_stages(_head_slabs(dq_ref, hpu), _head_slabs(dk_ref, hpu), _head_slabs(dv_ref, hpu),
                                  gb_ref[0], s_ref[0, pl.ds(h0, hpu)], hpu, dres))
    _interleave(*gens)
    m_sc[...] = res["m"]
    l_sc[...] = res["l"]
    acc_sc[...] = res["acc"]
    if delta is not None:
        slabs = _delta_finish(dres["o"], zb_ref, onw_ref, hpu)

        @pl.when(t < n_units)
        def _():
            s_ref[0, pl.ds(h0, hpu)] = dres["s"]
            for h, slab in enumerate(slabs):
                ob_ref[:, h * LANES:(h + 1) * LANES] = slab

    @pl.when(j == pl.num_programs(1) - 1)
    def _():
        lam = _lam(lamp_ref, lambda_init)
        o = acc_sc[...] / l_sc[...]
        for h in range(N_HEADS_A):
            oh = o[2 * h:2 * h + 1] - lam * o[2 * h + 1:2 * h + 2]
            o_ref[0, :, h * LANES:(h + 1) * LANES] = _subln(oh, sw_ref[...], lambda_init)


def _delta_units(bsz, s_len, n_steps):
    nc = s_len // CHUNK
    for hpu in (1, 2, 4, 8):
        if bsz * nc * (N_HEADS_B // hpu) <= n_steps:
            return hpu
    return None


def _attn_sample(q, kn, vn, cache_kt, cache_v, page_table, lamp, subln_w, lambda_init, delta_args=None):
    nb, n_pages = page_table.shape
    pps = PAGES_PER_STEP if n_pages % PAGES_PER_STEP == 0 else 1
    nj = n_pages // pps
    row_spec = pl.BlockSpec((1, 1, A_W), lambda b, j, pt: (b, 0, 0))
    d_in_specs, d_out_specs, d_out_shapes, d_args, delta = [], [], [], [], None
    if delta_args is not None:
        qkv, gb, zb, o_norm_w, bsz, s_len, hpu = delta_args
        nc = s_len // CHUNK
        ng = N_HEADS_B // hpu
        n_units = bsz * nc * ng
        delta = (hpu, ng, nc, n_units)
        unit = lambda b, j: jnp.minimum(b * nj + j, n_units - 1)
        cols = lambda seg: pl.BlockSpec(
            (CHUNK, hpu * LANES), lambda b, j, pt, seg=seg: (unit(b, j) // ng, seg * ng + unit(b, j) % ng))
        d_in_specs = [cols(0), cols(1), cols(2),
                      pl.BlockSpec((1, CHUNK, LANES), lambda b, j, pt: (unit(b, j) % ng, unit(b, j) // ng, 0)),
                      cols(0),
                      pl.BlockSpec((1, LANES), lambda b, j, pt: (0, 0))]
        d_out_specs = [cols(0), pl.BlockSpec((1, N_HEADS_B, DK_B, DV_B),
                                             lambda b, j, pt: (unit(b, j) // (ng * nc), 0, 0, 0))]
        d_out_shapes = [jax.ShapeDtypeStruct((qkv.shape[0], B_W), BF16),
                        jax.ShapeDtypeStruct((bsz, N_HEADS_B, DK_B, DV_B), F32)]
        d_args = [qkv, qkv, qkv, _group_gates(gb, hpu), zb, o_norm_w.reshape(1, LANES)]

    def page_spec(p_i, shape):
        return pl.BlockSpec((1,) + shape, lambda b, j, pt: (pt[b, j * pps + p_i],) + (0,) * len(shape))

    expand = (jnp.arange(PAGE_SIZE)[:, None] == jnp.arange(PAGE_SIZE * N_HEADS_A)[None, :] // N_HEADS_A).astype(BF16)
    grid_spec = pltpu.PrefetchScalarGridSpec(
        num_scalar_prefetch=1,
        grid=(nb, nj),
        in_specs=[pl.BlockSpec(lamp.shape, lambda b, j, pt: (0, 0)), row_spec, row_spec, row_spec,
                  pl.BlockSpec((1, LANES), lambda b, j, pt: (0, 0)),
                  pl.BlockSpec(expand.shape, lambda b, j, pt: (0, 0))]
                 + [page_spec(p_i, (A_W, PAGE_SIZE)) for p_i in range(pps)]
                 + [page_spec(p_i, (PAGE_SIZE, N_HEADS_A, LANES)) for p_i in range(pps)]
                 + d_in_specs,
        out_specs=[row_spec] + d_out_specs,
        scratch_shapes=[pltpu.VMEM((2 * N_HEADS_A, 1), F32), pltpu.VMEM((2 * N_HEADS_A, 1), F32),
                        pltpu.VMEM((2 * N_HEADS_A, LANES), F32)],
    )
    outs = pl.pallas_call(
        functools.partial(_dec_attn_kernel, n_pages_step=pps, lambda_init=lambda_init, delta=delta),
        grid_spec=grid_spec,
        out_shape=[jax.ShapeDtypeStruct((nb, 1, A_W), F32)] + d_out_shapes,
        compiler_params=_cparams(("arbitrary", "arbitrary")),
        name="attn_sample",
    )(page_table, lamp, q, kn, vn, subln_w.reshape(1, LANES), expand,
      *([cache_kt] * pps), *([cache_v] * pps), *d_args)
    return outs if delta_args is not None else outs[0]


def _l2n(y):
    return y * lax.rsqrt(jnp.sum(y * y, axis=-1, keepdims=True) + EPS)


def _conv_kernel(x_ref, w_ref, o_ref, ext_ref, *, tm):
    seg = pl.program_id(1)
    t = pl.program_id(2)

    @pl.when(t == 0)
    def _():
        ext_ref[0:SUBLANES, :] = jnp.zeros((SUBLANES, ext_ref.shape[1]), F32)

    ext_ref[SUBLANES:SUBLANES + tm, :] = x_ref[...]
    w = w_ref[...]
    y = ext_ref[SUBLANES - 3:SUBLANES - 3 + tm, :] * w[0:1]
    for jj in range(1, CONV_W):
        off = SUBLANES - 3 + jj
        y = y + ext_ref[off:off + tm, :] * w[jj:jj + 1]
    y = _silu(y)
    qscale = jnp.where(seg == 0, DK_B ** -0.5, 1.0)
    for h in range(N_HEADS_B):
        sl = slice(h * LANES, (h + 1) * LANES)
        yh = y[:, sl]
        o_ref[:, sl] = jnp.where(seg < 2, _l2n(yh) * qscale, yh)
    ext_ref[0:SUBLANES, :] = ext_ref[tm:tm + SUBLANES, :]


def _conv_prompt(raw, conv_w, bsz, s_len, tm):
    m, cdim = raw.shape
    nt = s_len // tm
    return pl.pallas_call(
        functools.partial(_conv_kernel, tm=tm),
        grid=(bsz, cdim // B_W, nt),
        in_specs=[pl.BlockSpec((tm, B_W), lambda b, g, t: (b * nt + t, g)),
                  pl.BlockSpec((CONV_W, B_W), lambda b, g, t: (0, g))],
        out_specs=pl.BlockSpec((tm, B_W), lambda b, g, t: (b * nt + t, g)),
        out_shape=jax.ShapeDtypeStruct((m, cdim), F32),
        scratch_shapes=[pltpu.VMEM((tm + SUBLANES, B_W), F32)],
        compiler_params=_cparams(("parallel", "parallel", "arbitrary")),
        name="conv_prompt",
    )(raw, conv_w)


def _bdot(a, b):
    return lax.dot_general(a, b, (((2,), (1,)), ((0,), (0,))), preferred_element_type=F32)


def _bdot_nt(a, b):
    return lax.dot_general(a, b, (((2,), (2,)), ((0,), (0,))), preferred_element_type=F32)


def _bdot_tn(a, b):
    return lax.dot_general(a, b, (((1,), (1,)), ((0,), (0,))), preferred_element_type=F32)


def _split_bdot(a, b):
    ah = a.astype(BF16)
    al = (a - ah.astype(F32)).astype(BF16)
    bh = b.astype(BF16)
    bl = (b - bh.astype(F32)).astype(BF16)
    return _bdot(ah, bh) + _bdot(ah, bl) + _bdot(al, bh)


def _cumsum_rows(x):
    n = x.shape[0]
    row = lax.broadcasted_iota(jnp.int32, x.shape, 0)
    s = 1
    while s < n:
        x = x + jnp.where(row >= s, pltpu.roll(x, s, axis=0), 0.0)
        s *= 2
    return x


def _delta_stages(q, k, v, gb, s_old, nh, res):
    lane_bcast = lambda x, off: jnp.stack(
        [jnp.broadcast_to(x[:, off + h:off + h + 1], (CHUNK, LANES)) for h in range(nh)])
    gc_all = _cumsum_rows(gb)
    gc_t = gc_all.T
    gc = lane_bcast(gc_all, 0)
    gc_row = jnp.stack([gc_t[h:h + 1, :] for h in range(nh)])
    beta = lane_bcast(gb, nh)
    row = lax.broadcasted_iota(jnp.int32, (1, CHUNK, CHUNK), 1)
    col = lax.broadcasted_iota(jnp.int32, (1, CHUNK, CHUNK), 2)
    tril = col <= row
    strict = col < row
    eye = jnp.where(row == col, 1.0, 0.0)
    decay = jnp.exp(jnp.where(tril, gc[:, :, :CHUNK] - gc_row, -jnp.inf))
    kb = k * beta
    k16 = k.astype(BF16)
    lmat = jnp.where(strict, _bdot_nt(kb.astype(BF16), k16) * decay, 0.0)
    qk = jnp.where(tril, _bdot_nt(q.astype(BF16), k16) * decay, 0.0)
    yield
    tinv = eye - lmat
    pw = lmat
    for _ in range(5):
        pw = _split_bdot(pw, pw)
        yield
        tinv = tinv + _split_bdot(tinv, pw)
        yield
    t16 = tinv.astype(BF16)
    u = _bdot(t16, (v * beta).astype(BF16))
    wk = _bdot(t16, (kb * jnp.exp(gc)).astype(BF16))
    s16 = s_old.astype(BF16)
    yield
    v_new = u - _bdot(wk.astype(BF16), s16)
    vn16 = v_new.astype(BF16)
    yield
    res["o"] = _bdot((q * jnp.exp(gc)).astype(BF16), s16) + _bdot(qk.astype(BF16), vn16)
    g_last = gc[:, CHUNK - 1:CHUNK, :]
    k_dec = k * jnp.exp(g_last - gc)
    res["s"] = s_old * jnp.exp(g_last) + _bdot_tn(k_dec.astype(BF16), vn16)


def _interleave(*gens):
    live = list(gens)
    while live:
        for g in list(live):
            try:
                next(g)
            except StopIteration:
                live.remove(g)


def _head_slabs(ref, nh):
    return jnp.stack([ref[:, h * LANES:(h + 1) * LANES] for h in range(nh)])


def _delta_finish(o, zb_ref, onw_ref, nh):
    on = o * lax.rsqrt(jnp.mean(o * o, axis=-1, keepdims=True) + EPS) * onw_ref[...]
    return [(on[h] * zb_ref[:, h * LANES:(h + 1) * LANES]).astype(BF16) for h in range(nh)]


def _delta_kernel(q_ref, k_ref, v_ref, gb_ref, zb_ref, onw_ref, ob_ref, s_ref):
    c = pl.program_id(1)
    nh = N_HEADS_B

    @pl.when(c == 0)
    def _():
        s_ref[...] = jnp.zeros(s_ref.shape, F32)

    res = {}
    _interleave(_delta_stages(_head_slabs(q_ref, nh), _head_slabs(k_ref, nh), _head_slabs(v_ref, nh),
                              gb_ref[0], s_ref[0], nh, res))
    s_ref[0] = res["s"]
    for h, slab in enumerate(_delta_finish(res["o"], zb_ref, onw_ref, nh)):
        ob_ref[:, h * LANES:(h + 1) * LANES] = slab


def _group_gates(gb, hpu):
    m = gb.shape[0]
    ng = N_HEADS_B // hpu
    g = gb[:, :N_HEADS_B].reshape(m, ng, hpu)
    beta = gb[:, N_HEADS_B:2 * N_HEADS_B].reshape(m, ng, hpu)
    gbg = jnp.transpose(jnp.concatenate([g, beta], axis=-1), (1, 0, 2))
    return jnp.pad(gbg, ((0, 0), (0, 0), (0, LANES - 2 * hpu)))


def _delta_prompt(qkv, gb, zb, o_norm_w, bsz, s_len):
    m = qkv.shape[0]
    nc = s_len // CHUNK
    seg = lambda g: pl.BlockSpec((CHUNK, B_W), lambda b, c, g=g: (b * nc + c, g))
    return pl.pallas_call(
        _delta_kernel,
        grid=(bsz, nc),
        in_specs=[seg(0), seg(1), seg(2),
                  pl.BlockSpec((1, CHUNK, LANES), lambda b, c: (0, b * nc + c, 0)),
                  seg(0),
                  pl.BlockSpec((1, LANES), lambda b, c: (0, 0))],
        out_specs=[seg(0), pl.BlockSpec((1, N_HEADS_B, DK_B, DV_B), lambda b, c: (b, 0, 0, 0))],
        out_shape=[jax.ShapeDtypeStruct((m, B_W), BF16),
                   jax.ShapeDtypeStruct((bsz, N_HEADS_B, DK_B, DV_B), F32)],
        compiler_params=_cparams(("parallel", "arbitrary")),
        name="delta_prompt",
    )(qkv, qkv, qkv, _group_gates(gb, N_HEADS_B), zb, o_norm_w.reshape(1, LANES))


def _dec_delta_kernel(u_ref, buf_ref, w_ref, gb_ref, zb_ref, onw_ref, s0_ref, ob_ref, s_ref):
    w = w_ref[...]
    buf = buf_ref[0]
    y = buf[0:1] * w[0:1] + buf[1:2] * w[1:2] + buf[2:3] * w[2:3] + u_ref[0] * w[3:4]
    y = _silu(y)
    gb = gb_ref[0]
    r = lax.broadcasted_iota(jnp.int32, (DK_B, LANES), 0)
    cidx = lax.broadcasted_iota(jnp.int32, (DK_B, LANES), 1)
    eye = r == cidx

    def column(x):
        return jnp.sum(jnp.where(eye, jnp.broadcast_to(x, (DK_B, LANES)), 0.0), axis=-1, keepdims=True)

    for h in range(N_HEADS_B):
        sl = slice(h * LANES, (h + 1) * LANES)
        q = _l2n(y[:, h * LANES:(h + 1) * LANES]) * (DK_B ** -0.5)
        k = _l2n(y[:, B_W + h * LANES:B_W + (h + 1) * LANES])
        v = y[:, 2 * B_W + h * LANES:2 * B_W + (h + 1) * LANES]
        a = jnp.exp(gb[:, h:h + 1])
        beta = gb[:, N_HEADS_B + h:N_HEADS_B + h + 1]
        s_old = s0_ref[0, h]
        k_col = column(k)
        q_col = column(q)
        v_new = beta * (v - a * jnp.sum(k_col * s_old, axis=0, keepdims=True))
        s_new = a * s_old + k_col * v_new
        s_ref[0, h] = s_new
        o = jnp.sum(q_col * s_new, axis=0, keepdims=True)
        on = o * lax.rsqrt(jnp.mean(o * o, axis=-1, keepdims=True) + EPS) * onw_ref[...]
        ob_ref[0, :, sl] = (on * zb_ref[0, :, sl]).astype(BF16)


def _delta_sample(u, conv_buf, conv_w, gb, zb, o_norm_w, s0):
    nb, _, cdim = u.shape
    row = lambda wd: pl.BlockSpec((1, 1, wd), lambda b: (b, 0, 0))
    st = pl.BlockSpec((1, N_HEADS_B, DK_B, DV_B), lambda b: (b, 0, 0, 0))
    return pl.pallas_call(
        _dec_delta_kernel,
        grid=(nb,),
        in_specs=[row(cdim), pl.BlockSpec((1, CONV_W - 1, cdim), lambda b: (b, 0, 0)),
                  pl.BlockSpec((CONV_W, cdim), lambda b: (0, 0)), row(LANES), row(B_W),
                  pl.BlockSpec((1, LANES), lambda b: (0, 0)), st],
        out_specs=[row(B_W), st],
        out_shape=[jax.ShapeDtypeStruct((nb, 1, B_W), BF16),
                   jax.ShapeDtypeStruct(s0.shape, F32)],
        compiler_params=_cparams(("parallel",)),
        name="delta_sample",
    )(u, conv_buf, conv_w, gb, zb, o_norm_w.reshape(1, LANES), s0)


def _out_kernel(oa_ref, za_ref, ob_ref, ga_ref, gbt_ref, x_ref, gate_ref, wa_ref, wb_ref, wo_ref, y_ref):
    ya = _dot((oa_ref[...] * za_ref[...]).astype(BF16), wa_ref[...])
    yb = _dot(ob_ref[...], wb_ref[...])
    mrg = ga_ref[...] * ya + gbt_ref[...] * yb
    y_ref[0] = x_ref[0] + gate_ref[0] * _dot(mrg.astype(BF16), wo_ref[...])


def _out_proj(oa, za, ob, gates, x, gate, wa, wb, wo, tm):
    b, s, d = x.shape
    nt = s // tm
    ts = tm if gate.shape[1] == s else 1
    gate_map = (lambda i, j: (i, j, 0)) if ts == tm else (lambda i, j: (i, 0, 0))
    rows = lambda wd, cb=0: pl.BlockSpec((tm, wd), lambda i, j, cb=cb: (i * nt + j, cb))
    const = lambda shp: pl.BlockSpec(shp, lambda i, j: (0, 0), pipeline_mode=pl.Buffered(1))
    return pl.pallas_call(
        _out_kernel,
        grid=(b, nt),
        in_specs=[rows(A_W), rows(A_W), rows(B_W), rows(d, 0), rows(d, 1),
                  pl.BlockSpec((1, tm, d), lambda i, j: (i, j, 0)),
                  pl.BlockSpec((1, ts, d), gate_map),
                  const(wa.shape), const(wb.shape), const(wo.shape)],
        out_specs=pl.BlockSpec((1, tm, d), lambda i, j: (i, j, 0)),
        out_shape=jax.ShapeDtypeStruct((b, s, d), F32),
        compiler_params=_cparams(("parallel", "parallel")),
        name="out_proj",
    )(oa, za, ob, gates, gates, x, gate, wa, wb, wo)


def _tile(n, pref):
    t = min(n, pref)
    while n % t:
        t //= 2
    return t


def _project(h2, wts, tm, kt_dims=None):
    (w_main, w_kt, w_g, cdim, gmat, qnw, knw, alog, dtb) = wts
    tn = A_W
    seg = lambda col, n: (w_main, col // tn, n)
    (q16,) = _proj("q", h2, seg(0, A_W), [gmat, qnw], [BF16], tm, tn)
    if kt_dims is None:
        k32, k16 = _proj("k", h2, seg(A_W, A_W), [gmat, knw], [F32, BF16], tm, tn)
    else:
        knw_col = knw[0, :HEAD_DIM_A].reshape(1, HEAD_DIM_A, 1)
        k32, k16 = _proj_kt(h2, w_kt, knw_col, kt_dims[0], kt_dims[1], tm)
    v32, v16 = _proj("v", h2, seg(2 * A_W, A_W), [], [F32, BF16], tm, tn)
    (za,) = _proj("silu", h2, seg(3 * A_W, A_W), [], [F32], tm, tn)
    (raw,) = _proj("raw", h2, seg(4 * A_W, cdim), [], [F32], tm, tn)
    (zb,) = _proj("silu", h2, seg(4 * A_W + cdim, B_W), [], [F32], tm, tn)
    (gb,) = _proj("ab", h2, (w_main, (4 * A_W + cdim + B_W) // LANES, LANES), [alog, dtb], [F32], tm, LANES)
    (gates,) = _proj("sigmoid", h2, (w_g, 0, w_g.shape[1]), [], [F32], tm, tn)
    return q16, k32, k16, v32, v16, za, raw, zb, gb, gates


def kernel(x_prompt, x_sample, c_prompt, c_sample, cache_k, cache_v, page_table, state_ssm, state_conv, w_ada, b_ada, norm_w, w_in, q_norm_w, k_norm_w, lam_q1, lam_k1, lam_q2, lam_k2, subln_w, conv_w, a_log, dt_bias, o_norm_w, w_branch_a, w_branch_b, w_out):
    depth = w_ada.shape[0]
    bsz, s_len, d = x_prompt.shape
    nb, t_dec, _ = x_sample.shape
    assert t_dec == 1 and s_len % CHUNK == 0 and s_len >= CONV_W - 1
    m_p = bsz * s_len
    cdim = conv_w.shape[-1]
    o_k, o_v = A_W, 2 * A_W
    o_ab = 4 * A_W + cdim + B_W
    o_g = o_ab + 2 * N_HEADS_B
    assert cdim % A_W == 0 and o_ab + LANES <= w_in.shape[-1]

    gi = jnp.arange(LANES) // HEAD_DIM_A
    gmat = jnp.where(gi[:, None] == gi[None, :], 1.0 / HEAD_DIM_A, 0.0).astype(BF16)

    yp, ys = x_prompt, x_sample
    outs = [[] for _ in range(8)]
    for li in range(depth):
        lambda_init = 0.8 - 0.6 * math.exp(-0.3 * li)
        wl = w_in[li]
        w_main = wl.astype(BF16)
        w_g = w_main[:, o_g:o_g + 2 * d]
        w_kt = w_main[:, o_k:o_v].T
        pad8 = lambda vec: jnp.pad(vec, (0, LANES - N_HEADS_B)).reshape(1, LANES)
        wts = (w_main, w_kt, w_g, cdim, gmat,
               jnp.tile(q_norm_w[li], 2 * N_HEADS_A).reshape(1, A_W),
               jnp.tile(k_norm_w[li], 2 * N_HEADS_A).reshape(1, A_W),
               pad8(a_log[li]), pad8(dt_bias[li]))
        wa16 = w_branch_a[li].astype(BF16)
        wbb16 = w_branch_b[li].astype(BF16)
        wo16 = w_out[li].astype(BF16)
        lamp = jnp.stack([lam_q1[li], lam_k1[li], lam_q2[li], lam_k2[li]])

        n_c = bsz + nb
        c_all = jnp.pad(jnp.concatenate([c_prompt, c_sample], axis=0), ((0, (-n_c) % 16), (0, 0)))
        mod = _ada(c_all, w_ada[li], b_ada[li])
        shift, scale, gate = mod[:, :d], mod[:, d:2 * d], mod[:, 2 * d:]

        tm = _tile(s_len, 1024)
        h = _hnorm(yp, shift[:bsz, None], scale[:bsz, None], norm_w[li], tm)
        q16, kt32, kt16, v32, v16, za, raw, zb, gb, gates = _project(h.reshape(m_p, d), wts, tm, (bsz, s_len))
        xs = ys.reshape(1, nb, d)
        sl_s = slice(bsz, bsz + nb)
        h_s = _hnorm(xs, shift[None, sl_s], scale[None, sl_s], norm_w[li], nb)
        q16_s, k32_s, _, v32_s, _, za_s, raw_s, zb_s, gb_s, gates_s = _project(h_s.reshape(nb, d), wts, nb)

        oa = _attn_prompt(q16, kt16, v16, lamp, subln_w[li], bsz, s_len, _tile(s_len, 512), lambda_init)
        qkv = _conv_prompt(raw, conv_w[li], bsz, s_len, _tile(s_len, 512))
        n_pool = cache_k.shape[1]
        cache_kt = jnp.transpose(cache_k[li], (0, 2, 3, 4, 1)).reshape(n_pool, A_W, PAGE_SIZE)
        dec_args = (q16_s.astype(F32).reshape(nb, 1, A_W), k32_s.reshape(nb, 1, A_W), v32_s.reshape(nb, 1, A_W),
                    cache_kt, cache_v[li], page_table, lamp, subln_w[li], lambda_init)
        n_pages = page_table.shape[1]
        n_steps = nb * (n_pages // (PAGES_PER_STEP if n_pages % PAGES_PER_STEP == 0 else 1))
        hpu = _delta_units(bsz, s_len, n_steps)
        if hpu is None:
            oa_s = _attn_sample(*dec_args)
            ob, ssm_p = _delta_prompt(qkv, gb, zb, o_norm_w[li], bsz, s_len)
        else:
            oa_s, ob, ssm_p = _attn_sample(*dec_args, delta_args=(qkv, gb, zb, o_norm_w[li], bsz, s_len, hpu))
        ob_s, ssm_s = _delta_sample(raw_s.reshape(nb, 1, cdim), state_conv[li], conv_w[li],
                                    gb_s.reshape(nb, 1, LANES), zb_s.reshape(nb, 1, B_W), o_norm_w[li],
                                    state_ssm[li])

        yp = _out_proj(oa, za, ob, gates, yp, gate[:bsz, None], wa16, wbb16, wo16, _tile(s_len, 256))
        ys = _out_proj(oa_s.reshape(nb, A_W), za_s, ob_s.reshape(nb, B_W), gates_s, xs, gate[None, sl_s],
                       wa16, wbb16, wo16, nb).reshape(nb, 1, d)
        outs[0].append(jnp.transpose(kt32.reshape(bsz, N_HEADS_A, 2, HEAD_DIM_A, s_len), (0, 4, 1, 2, 3)))
        outs[1].append(v32.reshape(bsz, s_len, N_HEADS_A, 2 * HEAD_DIM_A))
        outs[2].append(k32_s.reshape(nb, 1, N_HEADS_A, 2, HEAD_DIM_A))
        outs[3].append(v32_s.reshape(nb, 1, N_HEADS_A, 2 * HEAD_DIM_A))
        outs[4].append(ssm_p)
        outs[5].append(ssm_s)
        outs[6].append(raw.reshape(bsz, s_len, cdim)[:, s_len - (CONV_W - 1):])
        outs[7].append(jnp.concatenate([state_conv[li][:, 1:], raw_s.reshape(nb, 1, cdim)], axis=1))

    st = [jnp.stack(o) for o in outs]
    return (yp, ys, st[0], st[1], st[2], st[3], st[4], st[5], st[6], st[7])
```

```python
import functools
import math

import jax
import jax.numpy as jnp
from jax import lax
from jax.experimental import pallas as pl
from jax.experimental.pallas import tpu as pltpu

F32 = jnp.float32
BF16 = jnp.bfloat16

N_HEADS_A = 8
HEAD_DIM_A = 64
A_W = N_HEADS_A * 2 * HEAD_DIM_A
N_HEADS_B = 8
DK_B = 128
DV_B = 128
B_W = N_HEADS_B * DK_B
CONV_W = 4
CHUNK = 64
PAGE_SIZE = 128
EPS = 1e-6
LANES = 128
SUBLANES = 8
VMEM_LIMIT = 56 * 1024 * 1024
PAGES_PER_STEP = 16

_NT = (((1,), (1,)), ((), ()))
_TN = (((0,), (0,)), ((), ()))


def _cparams(sem):
    return pltpu.CompilerParams(dimension_semantics=sem, vmem_limit_bytes=VMEM_LIMIT)


def _dot(a, b):
    return jnp.dot(a, b, preferred_element_type=F32)


def _dot_nt(a, b):
    return lax.dot_general(a, b, _NT, preferred_element_type=F32)


def _dot_tn(a, b):
    return lax.dot_general(a, b, _TN, preferred_element_type=F32)


def _sigmoid(x):
    return 1.0 / (1.0 + jnp.exp(-x))


def _silu(x):
    return x * _sigmoid(x)


def _ada_kernel(c_ref, w_ref, b_ref, o_ref):
    sc = _silu(c_ref[...]).astype(BF16)
    o_ref[...] = _dot(sc, w_ref[...].astype(BF16)) + b_ref[...]


def _ada(c, w_ada, b_ada):
    m, d = c.shape
    n = w_ada.shape[1]
    tn = 512
    return pl.pallas_call(
        _ada_kernel,
        grid=(n // tn,),
        in_specs=[pl.BlockSpec((m, d), lambda j: (0, 0)),
                  pl.BlockSpec((d, tn), lambda j: (0, j)),
                  pl.BlockSpec((1, tn), lambda j: (0, j))],
        out_specs=pl.BlockSpec((m, tn), lambda j: (0, j)),
        out_shape=jax.ShapeDtypeStruct((m, n), F32),
        compiler_params=_cparams(("parallel",)),
        name="ada",
    )(c, w_ada, b_ada.reshape(1, n))


def _hnorm_kernel(x_ref, shift_ref, scale_ref, nw_ref, h_ref):
    x = x_ref[0]
    ms = jnp.mean(x * x, axis=-1, keepdims=True)
    y = x * lax.rsqrt(ms + EPS) * nw_ref[...]
    h_ref[0] = (y * (1.0 + scale_ref[0]) + shift_ref[0]).astype(BF16)


def _hnorm(x, shift, scale, norm_w, tm):
    b, s, d = x.shape
    ts = tm if shift.shape[1] == s else 1
    mod_map = (lambda i, j: (i, j, 0)) if ts == tm else (lambda i, j: (i, 0, 0))
    return pl.pallas_call(
        _hnorm_kernel,
        grid=(b, s // tm),
        in_specs=[pl.BlockSpec((1, tm, d), lambda i, j: (i, j, 0)),
                  pl.BlockSpec((1, ts, d), mod_map),
                  pl.BlockSpec((1, ts, d), mod_map),
                  pl.BlockSpec((1, d), lambda i, j: (0, 0))],
        out_specs=pl.BlockSpec((1, tm, d), lambda i, j: (i, j, 0)),
        out_shape=jax.ShapeDtypeStruct((b, s, d), BF16),
        compiler_params=_cparams(("parallel", "parallel")),
        name="hnorm",
    )(x, shift, scale, norm_w.reshape(1, d))


def _group_rms(acc, g_ref):
    sq = acc * acc
    hi = sq.astype(BF16)
    lo = (sq - hi.astype(F32)).astype(BF16)
    g = g_ref[...]
    outs = []
    for hh in range(acc.shape[1] // LANES):
        sl = slice(hh * LANES, (hh + 1) * LANES)
        ms = _dot(hi[:, sl], g) + _dot(lo[:, sl], g)
        outs.append(acc[:, sl] * lax.rsqrt(ms + EPS))
    return outs


def _proj_kernel(kind, h_ref, w_ref, *refs):
    *refs, w16_ref = refs

    @pl.when(pl.program_id(1) == 0)
    def _():
        w16_ref[...] = w_ref[...].astype(BF16)

    acc = _dot(h_ref[...], w16_ref[...])
    if kind == "q":
        g_ref, nw_ref, o_ref = refs
        for hh, y in enumerate(_group_rms(acc, g_ref)):
            sl = slice(hh * LANES, (hh + 1) * LANES)
            o_ref[:, sl] = (y * nw_ref[:, sl] * (HEAD_DIM_A ** -0.5)).astype(BF16)
    elif kind == "k":
        g_ref, nw_ref, o_ref, ob_ref = refs
        for hh, y in enumerate(_group_rms(acc, g_ref)):
            sl = slice(hh * LANES, (hh + 1) * LANES)
            y = y * nw_ref[:, sl]
            o_ref[:, sl] = y
            ob_ref[:, sl] = y.astype(BF16)
    elif kind == "v":
        o_ref, ob_ref = refs
        o_ref[...] = acc
        ob_ref[...] = acc.astype(BF16)
    elif kind == "silu":
        (o_ref,) = refs
        o_ref[...] = _silu(acc)
    elif kind == "sigmoid":
        (o_ref,) = refs
        o_ref[...] = _sigmoid(acc)
    elif kind == "raw":
        (o_ref,) = refs
        o_ref[...] = acc
    elif kind == "ab":
        alog_ref, dtb_ref, o_ref = refs
        a = acc + dtb_ref[...]
        sp = jnp.maximum(a, 0.0) + jnp.log(1.0 + jnp.exp(-jnp.abs(a)))
        g = -jnp.exp(alog_ref[...]) * sp
        lane = lax.broadcasted_iota(jnp.int32, acc.shape, 1)
        o_ref[...] = jnp.where(lane < N_HEADS_B, g, _sigmoid(acc))
    else:
        raise ValueError(kind)


def _proj_kt_kernel(h_ref, wt_ref, nw_ref, o_ref, ob_ref):
    acc = _dot_nt(wt_ref[...], h_ref[...])
    tm = acc.shape[1]
    x3 = acc.reshape(2 * N_HEADS_A, HEAD_DIM_A, tm)
    ms = jnp.mean(x3 * x3, axis=1, keepdims=True)
    y = (x3 * lax.rsqrt(ms + EPS) * nw_ref[...]).reshape(A_W, tm)
    o_ref[0] = y
    ob_ref[0] = y.astype(BF16)


def _proj_kt(h, wt, nw_col, bsz, s_len, tm):
    k = h.shape[1]
    nt = s_len // tm
    out_spec = pl.BlockSpec((1, A_W, tm), lambda i: (i // nt, 0, i % nt))
    return pl.pallas_call(
        _proj_kt_kernel,
        grid=(bsz * nt,),
        in_specs=[pl.BlockSpec((tm, k), lambda i: (i, 0)),
                  pl.BlockSpec((A_W, k), lambda i: (0, 0)),
                  pl.BlockSpec(nw_col.shape, lambda i: (0, 0, 0))],
        out_specs=[out_spec, out_spec],
        out_shape=[jax.ShapeDtypeStruct((bsz, A_W, s_len), F32),
                   jax.ShapeDtypeStruct((bsz, A_W, s_len), BF16)],
        compiler_params=_cparams(("parallel",)),
        name="proj_kt",
    )(h, wt, nw_col)


def _proj(kind, h, wseg, extras, out_dtypes, tm, tn):
    w, col0, n = wseg
    m, k = h.shape
    extra_specs = [pl.BlockSpec(e.shape, lambda j, i, nd=e.ndim: (0,) * nd) for e in extras]
    if kind in ("q", "k"):
        extra_specs[1] = pl.BlockSpec((1, tn), lambda j, i: (0, j))
    outs = pl.pallas_call(
        functools.partial(_proj_kernel, kind),
        grid=(n // tn, m // tm),
        in_specs=[pl.BlockSpec((tm, k), lambda j, i: (i, 0)),
                  pl.BlockSpec((k, tn), lambda j, i: (0, col0 + j))] + extra_specs,
        out_specs=[pl.BlockSpec((tm, tn), lambda j, i: (i, j)) for _ in out_dtypes],
        out_shape=[jax.ShapeDtypeStruct((m, n), dt) for dt in out_dtypes],
        scratch_shapes=[pltpu.VMEM((k, tn), BF16)],
        compiler_params=_cparams(("parallel", "arbitrary")),
        name="proj_" + kind,
    )(h, w, *extras)
    return outs


def _lam(lamp_ref, lambda_init):
    lp = lamp_ref[...]
    return (jnp.exp(jnp.sum(lp[0:1] * lp[1:2], axis=-1, keepdims=True))
            - jnp.exp(jnp.sum(lp[2:3] * lp[3:4], axis=-1, keepdims=True)) + lambda_init)


def _subln(o, sw, lambda_init):
    ms = jnp.mean(o * o, axis=-1, keepdims=True)
    return o * lax.rsqrt(ms + EPS) * sw * (1.0 - lambda_init)


def _attn_kernel(lamp_ref, q_ref, k_ref, v_ref, sw_ref, o_ref, m_sc, l_sc, acc_sc, *, tq, lambda_init):
    i = pl.program_id(2)
    q = q_ref[...]
    lane = lax.broadcasted_iota(jnp.int32, q.shape, 1)
    zero = jnp.zeros_like(q)
    qs = (jnp.where(lane < HEAD_DIM_A, q, zero), jnp.where(lane >= HEAD_DIM_A, q, zero))
    m_sc[...] = jnp.full(m_sc.shape, -jnp.inf, F32)
    l_sc[...] = jnp.zeros(l_sc.shape, F32)
    acc_sc[...] = jnp.zeros(acc_sc.shape, F32)

    def step(j, masked):
        start = pl.multiple_of(j * tq, tq)
        kb = k_ref[0, :, pl.ds(start, tq)]
        vb = v_ref[pl.ds(start, tq), :]
        for c in range(2):
            s = _dot(qs[c], kb)
            if masked:
                row = lax.broadcasted_iota(jnp.int32, s.shape, 0)
                col = lax.broadcasted_iota(jnp.int32, s.shape, 1)
                s = jnp.where(col <= row, s, -jnp.inf)
            m_prev = m_sc[c]
            m_new = jnp.maximum(m_prev, jnp.max(s, axis=-1, keepdims=True))
            alpha = jnp.exp(m_prev - m_new)
            ps = [jnp.exp(s[:, g * LANES:(g + 1) * LANES] - m_new) for g in range(tq // LANES)]
            l_sc[c] = alpha * l_sc[c] + functools.reduce(lambda a, b: a + b, ps)
            p = jnp.concatenate([x.astype(BF16) for x in ps], axis=1)
            acc_sc[c] = alpha * acc_sc[c] + _dot(p, vb)
            m_sc[c] = m_new

    def body(j, carry):
        step(j, False)
        return carry

    lax.fori_loop(0, i, body, 0)
    step(i, True)
    lam = _lam(lamp_ref, lambda_init)
    l0 = jnp.sum(l_sc[0], axis=-1, keepdims=True)
    l1 = jnp.sum(l_sc[1], axis=-1, keepdims=True)
    o = acc_sc[0] / l0 - lam * (acc_sc[1] / l1)
    o_ref[...] = _subln(o, sw_ref[...], lambda_init)


def _attn_prompt(q, k, v, lamp, subln_w, bsz, s_len, tq, lambda_init):
    m = q.shape[0]
    nq = s_len // tq
    return pl.pallas_call(
        functools.partial(_attn_kernel, tq=tq, lambda_init=lambda_init),
        grid=(bsz, N_HEADS_A, nq),
        in_specs=[pl.BlockSpec(lamp.shape, lambda b, h, i: (0, 0)),
                  pl.BlockSpec((tq, LANES), lambda b, h, i: (b * nq + i, h)),
                  pl.BlockSpec((1, LANES, s_len), lambda b, h, i: (b, h, 0)),
                  pl.BlockSpec((s_len, LANES), lambda b, h, i: (b, h)),
                  pl.BlockSpec((1, LANES), lambda b, h, i: (0, 0))],
        out_specs=pl.BlockSpec((tq, LANES), lambda b, h, i: (b * nq + i, h)),
        out_shape=jax.ShapeDtypeStruct((m, A_W), F32),
        scratch_shapes=[pltpu.VMEM((2, tq, LANES), F32)] * 3,
        compiler_params=_cparams(("parallel", "parallel", "parallel")),
        name="attn_prompt",
    )(lamp, q, k, v, subln_w.reshape(1, LANES))


def _decode_stages(qbd, e_ref, k_refs, v_refs, state, res):
    m_prev, l_prev, acc_prev = state
    parts = []
    pair = 2 if len(k_refs) % 2 == 0 else 1
    for p_i in range(0, len(k_refs), pair):
        kt = jnp.concatenate([k_refs[p_i + x][0].astype(BF16) for x in range(pair)], axis=1)
        parts.append(_dot(qbd, kt))
        yield
    s = jnp.concatenate(parts, axis=1)
    m_new = jnp.maximum(m_prev, jnp.max(s, axis=-1, keepdims=True))
    alpha = jnp.exp(m_prev - m_new)
    p = jnp.exp(s - m_new)
    res["m"] = m_new
    res["l"] = alpha * l_prev + jnp.sum(p, axis=-1, keepdims=True)
    nrow = p.shape[0]
    p_rows = jnp.concatenate([p[:, i * PAGE_SIZE:(i + 1) * PAGE_SIZE] for i in range(len(v_refs))], axis=0)
    pe_all = _dot(p_rows.astype(BF16), e_ref[...])
    yield
    pv = jnp.zeros(acc_prev.shape, F32)
    r_i = lax.broadcasted_iota(jnp.int32, (nrow, PAGE_SIZE * N_HEADS_A), 0)
    c_i = lax.broadcasted_iota(jnp.int32, (nrow, PAGE_SIZE * N_HEADS_A), 1)
    own = c_i % N_HEADS_A == r_i // 2
    for p_i, v_ref in enumerate(v_refs):
        pe = jnp.where(own, pe_all[p_i * nrow:(p_i + 1) * nrow], 0.0).astype(BF16)
        v2 = v_ref[0].reshape(PAGE_SIZE * N_HEADS_A, LANES).astype(BF16)
        pv = pv + _dot(pe, v2)
        yield
    res["acc"] = alpha * acc_prev + pv


def _dec_attn_kernel(pt_ref, lamp_ref, q_ref, kn_ref, vn_ref, sw_ref, e_ref, *refs, n_pages_step, lambda_init,
                     delta):
    k_refs = refs[:n_pages_step]
    v_refs = refs[n_pages_step:2 * n_pages_step]
    rest = refs[2 * n_pages_step:]
    if delta is None:
        o_ref, m_sc, l_sc, acc_sc = rest
    else:
        d_in = rest[:N_DELTA_IN]
        gb_ref, zb_ref, onw_ref = d_in[9:]
        o_ref, ob_ref, s_ref, m_sc, l_sc, acc_sc, ext_ref = rest[N_DELTA_IN:]
    j = pl.program_id(1)
    nrow = 2 * N_HEADS_A
    q = q_ref[0]
    hc = lax.broadcasted_iota(jnp.int32, (nrow, A_W), 0)
    lane = lax.broadcasted_iota(jnp.int32, (nrow, A_W), 1)
    qbd = jnp.where(lane // HEAD_DIM_A == hc, jnp.broadcast_to(q, (nrow, A_W)), 0.0).astype(BF16)

    @pl.when(j == 0)
    def _():
        kn = jnp.broadcast_to(kn_ref[0], (nrow, A_W)).astype(BF16).astype(F32)
        m_sc[...] = jnp.sum(qbd.astype(F32) * kn, axis=-1, keepdims=True)
        l_sc[...] = jnp.ones(l_sc.shape, F32)
        vn = vn_ref[0].astype(BF16).astype(F32)
        row_head = lax.broadcasted_iota(jnp.int32, (nrow, LANES), 0) // 2
        acc = jnp.zeros((nrow, LANES), F32)
        for h in range(N_HEADS_A):
            acc = jnp.where(row_head == h, jnp.broadcast_to(vn[:, h * LANES:(h + 1) * LANES], (nrow, LANES)), acc)
        acc_sc[...] = acc

    res = {}
    gens = [_decode_stages(qbd, e_ref, k_refs, v_refs, (m_sc[...], l_sc[...], acc_sc[...]), res)]
    if delta is not None:
        hpu, ng, nc, n_units = delta
        t = pl.program_id(0) * pl.num_programs(1) + j
        unit = jnp.minimum(t, n_units - 1)
        h0 = pl.multiple_of((unit % ng) * hpu, hpu)

        first_chunk = (unit // ng) % nc == 0

        @pl.when(first_chunk & (t < n_units))
        def _():
            s_ref[0, pl.ds(h0, hpu)] = jnp.zeros((hpu, DK_B, DV_B), F32)

        dres = {}

        def conv_then_delta():
            dq, dk, dv = _conv_unit(d_in[0:3], d_in[3:6], d_in[6:9], ext_ref, first_chunk, hpu)
            yield
            yield from _delta_stages(dq, dk, dv, gb_ref[0], s_ref[0, pl.ds(h0, hpu)], hpu, dres)

        gens.append(conv_then_delta())
    _interleave(*gens)
    m_sc[...] = res["m"]
    l_sc[...] = res["l"]
    acc_sc[...] = res["acc"]
    if delta is not None:
        slabs = _delta_finish(dres["o"], zb_ref, onw_ref, hpu)

        @pl.when(t < n_units)
        def _():
            s_ref[0, pl.ds(h0, hpu)] = dres["s"]
            for h, slab in enumerate(slabs):
                ob_ref[:, h * LANES:(h + 1) * LANES] = slab

    @pl.when(j == pl.num_programs(1) - 1)
    def _():
        lam = _lam(lamp_ref, lambda_init)
        o = acc_sc[...] / l_sc[...]
        for h in range(N_HEADS_A):
            oh = o[2 * h:2 * h + 1] - lam * o[2 * h + 1:2 * h + 2]
            o_ref[0, :, h * LANES:(h + 1) * LANES] = _subln(oh, sw_ref[...], lambda_init)


def _delta_units(bsz, s_len, n_steps):
    nc = s_len // CHUNK
    for hpu in (1, 2, 4, 8):
        if bsz * nc * (N_HEADS_B // hpu) <= n_steps:
            return hpu
    return None


def _attn_sample(q, kn, vn, cache_kt, cache_v, page_table, lamp, subln_w, lambda_init, delta_args=None):
    nb, n_pages = page_table.shape
    pps = PAGES_PER_STEP if n_pages % PAGES_PER_STEP == 0 else 1
    nj = n_pages // pps
    row_spec = pl.BlockSpec((1, 1, A_W), lambda b, j, pt: (b, 0, 0))
    d_in_specs, d_out_specs, d_out_shapes, d_args, d_scratch, delta = [], [], [], [], [], None
    if delta_args is not None:
        raw, conv_w, gb, zb, o_norm_w, bsz, s_len, hpu = delta_args
        nc = s_len // CHUNK
        ng = N_HEADS_B // hpu
        n_units = bsz * nc * ng
        delta = (hpu, ng, nc, n_units)
        unit = lambda b, j: jnp.minimum(b * nj + j, n_units - 1)
        d_in_specs, d_args, ob_spec, ext = _delta_operands(
            raw, conv_w, gb, zb, o_norm_w, hpu, lambda b, j, pt: (unit(b, j) // ng, unit(b, j) % ng))
        d_out_specs = [ob_spec, pl.BlockSpec((1, N_HEADS_B, DK_B, DV_B),
                                             lambda b, j, pt: (unit(b, j) // (ng * nc), 0, 0, 0))]
        d_out_shapes = [jax.ShapeDtypeStruct((raw.shape[0], B_W), BF16),
                        jax.ShapeDtypeStruct((bsz, N_HEADS_B, DK_B, DV_B), F32)]
        d_scratch = [ext]

    def page_spec(p_i, shape):
        return pl.BlockSpec((1,) + shape, lambda b, j, pt: (pt[b, j * pps + p_i],) + (0,) * len(shape))

    expand = (jnp.arange(PAGE_SIZE)[:, None] == jnp.arange(PAGE_SIZE * N_HEADS_A)[None, :] // N_HEADS_A).astype(BF16)
    grid_spec = pltpu.PrefetchScalarGridSpec(
        num_scalar_prefetch=1,
        grid=(nb, nj),
        in_specs=[pl.BlockSpec(lamp.shape, lambda b, j, pt: (0, 0)), row_spec, row_spec, row_spec,
                  pl.BlockSpec((1, LANES), lambda b, j, pt: (0, 0)),
                  pl.BlockSpec(expand.shape, lambda b, j, pt: (0, 0))]
                 + [page_spec(p_i, (A_W, PAGE_SIZE)) for p_i in range(pps)]
                 + [page_spec(p_i, (PAGE_SIZE, N_HEADS_A, LANES)) for p_i in range(pps)]
                 + d_in_specs,
        out_specs=[row_spec] + d_out_specs,
        scratch_shapes=[pltpu.VMEM((2 * N_HEADS_A, 1), F32), pltpu.VMEM((2 * N_HEADS_A, 1), F32),
                        pltpu.VMEM((2 * N_HEADS_A, LANES), F32)] + d_scratch,
    )
    outs = pl.pallas_call(
        functools.partial(_dec_attn_kernel, n_pages_step=pps, lambda_init=lambda_init, delta=delta),
        grid_spec=grid_spec,
        out_shape=[jax.ShapeDtypeStruct((nb, 1, A_W), F32)] + d_out_shapes,
        compiler_params=_cparams(("arbitrary", "arbitrary")),
        name="attn_sample",
    )(page_table, lamp, q, kn, vn, subln_w.reshape(1, LANES), expand,
      *([cache_kt] * pps), *([cache_v] * pps), *d_args)
    return outs if delta_args is not None else outs[0]


def _l2n(y):
    return y * lax.rsqrt(jnp.sum(y * y, axis=-1, keepdims=True) + EPS)


def _conv_unit(raw_refs, prev_refs, w_refs, ext_ref, first_chunk, nh):
    hist = SUBLANES - (CONV_W - 1)
    outs = []
    for s in range(3):
        ext_ref[s, 0:SUBLANES, :] = jnp.where(first_chunk, 0.0, prev_refs[s][...])
        ext_ref[s, SUBLANES:SUBLANES + CHUNK, :] = raw_refs[s][...]
        w = w_refs[s][...]
        y = ext_ref[s, hist:hist + CHUNK, :] * w[0:1]
        for jj in range(1, CONV_W):
            y = y + ext_ref[s, hist + jj:hist + jj + CHUNK, :] * w[jj:jj + 1]
        y = _silu(y)
        slabs = [y[:, h * LANES:(h + 1) * LANES] for h in range(nh)]
        if s == 0:
            slabs = [_l2n(x) * (DK_B ** -0.5) for x in slabs]
        elif s == 1:
            slabs = [_l2n(x) for x in slabs]
        outs.append(jnp.stack(slabs))
    return outs


def _bdot(a, b):
    return lax.dot_general(a, b, (((2,), (1,)), ((0,), (0,))), preferred_element_type=F32)


def _bdot_nt(a, b):
    return lax.dot_general(a, b, (((2,), (2,)), ((0,), (0,))), preferred_element_type=F32)


def _bdot_tn(a, b):
    return lax.dot_general(a, b, (((1,), (1,)), ((0,), (0,))), preferred_element_type=F32)


def _split_bdot(a, b):
    ah = a.astype(BF16)
    al = (a - ah.astype(F32)).astype(BF16)
    bh = b.astype(BF16)
    bl = (b - bh.astype(F32)).astype(BF16)
    return _bdot(ah, bh) + _bdot(ah, bl) + _bdot(al, bh)


def _cumsum_rows(x):
    n = x.shape[0]
    row = lax.broadcasted_iota(jnp.int32, x.shape, 0)
    s = 1
    while s < n:
        x = x + jnp.where(row >= s, pltpu.roll(x, s, axis=0), 0.0)
        s *= 2
    return x


def _delta_stages(q, k, v, gb, s_old, nh, res):
    lane_bcast = lambda x, off: jnp.stack(
        [jnp.broadcast_to(x[:, off + h:off + h + 1], (CHUNK, LANES)) for h in range(nh)])
    gc_all = _cumsum_rows(gb)
    gc_t = gc_all.T
    gc = lane_bcast(gc_all, 0)
    gc_row = jnp.stack([gc_t[h:h + 1, :] for h in range(nh)])
    beta = lane_bcast(gb, nh)
    row = lax.broadcasted_iota(jnp.int32, (1, CHUNK, CHUNK), 1)
    col = lax.broadcasted_iota(jnp.int32, (1, CHUNK, CHUNK), 2)
    tril = col <= row
    strict = col < row
    eye = jnp.where(row == col, 1.0, 0.0)
    decay = jnp.exp(jnp.where(tril, gc[:, :, :CHUNK] - gc_row, -jnp.inf))
    kb = k * beta
    k16 = k.astype(BF16)
    lmat = jnp.where(strict, _bdot_nt(kb.astype(BF16), k16) * decay, 0.0)
    qk = jnp.where(tril, _bdot_nt(q.astype(BF16), k16) * decay, 0.0)
    yield
    tinv = eye - lmat
    pw = lmat
    for _ in range(5):
        pw = _split_bdot(pw, pw)
        yield
        tinv = tinv + _split_bdot(tinv, pw)
        yield
    t16 = tinv.astype(BF16)
    u = _bdot(t16, (v * beta).astype(BF16))
    wk = _bdot(t16, (kb * jnp.exp(gc)).astype(BF16))
    s16 = s_old.astype(BF16)
    yield
    v_new = u - _bdot(wk.astype(BF16), s16)
    vn16 = v_new.astype(BF16)
    yield
    res["o"] = _bdot((q * jnp.exp(gc)).astype(BF16), s16) + _bdot(qk.astype(BF16), vn16)
    g_last = gc[:, CHUNK - 1:CHUNK, :]
    k_dec = k * jnp.exp(g_last - gc)
    res["s"] = s_old * jnp.exp(g_last) + _bdot_tn(k_dec.astype(BF16), vn16)


def _interleave(*gens):
    live = list(gens)
    while live:
        for g in list(live):
            try:
                next(g)
            except StopIteration:
                live.remove(g)


def _delta_finish(o, zb_ref, onw_ref, nh):
    on = o * lax.rsqrt(jnp.mean(o * o, axis=-1, keepdims=True) + EPS) * onw_ref[...]
    return [(on[h] * zb_ref[:, h * LANES:(h + 1) * LANES]).astype(BF16) for h in range(nh)]


N_DELTA_IN = 12


def _delta_kernel(*refs):
    d_in = refs[:N_DELTA_IN]
    ob_ref, s_ref, ext_ref = refs[N_DELTA_IN:]
    gb_ref, zb_ref, onw_ref = d_in[9:]
    c = pl.program_id(1)
    nh = N_HEADS_B

    @pl.when(c == 0)
    def _():
        s_ref[...] = jnp.zeros(s_ref.shape, F32)

    q, k, v = _conv_unit(d_in[0:3], d_in[3:6], d_in[6:9], ext_ref, c == 0, nh)
    res = {}
    _interleave(_delta_stages(q, k, v, gb_ref[0], s_ref[0], nh, res))
    s_ref[0] = res["s"]
    for h, slab in enumerate(_delta_finish(res["o"], zb_ref, onw_ref, nh)):
        ob_ref[:, h * LANES:(h + 1) * LANES] = slab


def _delta_operands(raw, conv_w, gb, zb, o_norm_w, hpu, where):
    ng = N_HEADS_B // hpu
    wd = hpu * LANES
    rc = lambda *a: where(*a)[0]
    hg = lambda *a: where(*a)[1]
    cur = lambda s: pl.BlockSpec((CHUNK, wd), lambda *a, s=s: (rc(*a), s * ng + hg(*a)))
    prev = lambda s: pl.BlockSpec(
        (SUBLANES, wd), lambda *a, s=s: (jnp.maximum(rc(*a) * (CHUNK // SUBLANES) - 1, 0), s * ng + hg(*a)))
    taps = lambda s: pl.BlockSpec((CONV_W, wd), lambda *a, s=s: (0, s * ng + hg(*a)))
    specs = ([cur(s) for s in range(3)] + [prev(s) for s in range(3)] + [taps(s) for s in range(3)]
             + [pl.BlockSpec((1, CHUNK, LANES), lambda *a: (hg(*a), rc(*a), 0)),
                pl.BlockSpec((CHUNK, wd), lambda *a: (rc(*a), hg(*a))),
                pl.BlockSpec((1, LANES), lambda *a: (0, 0))])
    args = [raw] * 6 + [conv_w] * 3 + [_group_gates(gb, hpu), zb, o_norm_w.reshape(1, LANES)]
    out_spec = pl.BlockSpec((CHUNK, wd), lambda *a: (rc(*a), hg(*a)))
    scratch = pltpu.VMEM((3, CHUNK + SUBLANES, wd), F32)
    return specs, args, out_spec, scratch


def _group_gates(gb, hpu):
    m = gb.shape[0]
    ng = N_HEADS_B // hpu
    g = gb[:, :N_HEADS_B].reshape(m, ng, hpu)
    beta = gb[:, N_HEADS_B:2 * N_HEADS_B].reshape(m, ng, hpu)
    gbg = jnp.transpose(jnp.concatenate([g, beta], axis=-1), (1, 0, 2))
    return jnp.pad(gbg, ((0, 0), (0, 0), (0, LANES - 2 * hpu)))


def _delta_prompt(raw, conv_w, gb, zb, o_norm_w, bsz, s_len):
    m = raw.shape[0]
    nc = s_len // CHUNK
    specs, args, out_spec, scratch = _delta_operands(raw, conv_w, gb, zb, o_norm_w, N_HEADS_B,
                                                     lambda b, c: (b * nc + c, 0))
    return pl.pallas_call(
        _delta_kernel,
        grid=(bsz, nc),
        in_specs=specs,
        out_specs=[out_spec, pl.BlockSpec((1, N_HEADS_B, DK_B, DV_B), lambda b, c: (b, 0, 0, 0))],
        out_shape=[jax.ShapeDtypeStruct((m, B_W), BF16),
                   jax.ShapeDtypeStruct((bsz, N_HEADS_B, DK_B, DV_B), F32)],
        scratch_shapes=[scratch],
        compiler_params=_cparams(("parallel", "arbitrary")),
        name="delta_prompt",
    )(*args)


def _dec_delta_kernel(u_ref, buf_ref, w_ref, gb_ref, zb_ref, onw_ref, s0_ref, ob_ref, s_ref):
    w = w_ref[...]
    buf = buf_ref[0]
    y = buf[0:1] * w[0:1] + buf[1:2] * w[1:2] + buf[2:3] * w[2:3] + u_ref[0] * w[3:4]
    y = _silu(y)
    gb = gb_ref[0]
    r = lax.broadcasted_iota(jnp.int32, (DK_B, LANES), 0)
    cidx = lax.broadcasted_iota(jnp.int32, (DK_B, LANES), 1)
    eye = r == cidx

    def column(x):
        return jnp.sum(jnp.where(eye, jnp.broadcast_to(x, (DK_B, LANES)), 0.0), axis=-1, keepdims=True)

    for h in range(N_HEADS_B):
        sl = slice(h * LANES, (h + 1) * LANES)
        q = _l2n(y[:, h * LANES:(h + 1) * LANES]) * (DK_B ** -0.5)
        k = _l2n(y[:, B_W + h * LANES:B_W + (h + 1) * LANES])
        v = y[:, 2 * B_W + h * LANES:2 * B_W + (h + 1) * LANES]
        a = jnp.exp(gb[:, h:h + 1])
        beta = gb[:, N_HEADS_B + h:N_HEADS_B + h + 1]
        s_old = s0_ref[0, h]
        k_col = column(k)
        q_col = column(q)
        v_new = beta * (v - a * jnp.sum(k_col * s_old, axis=0, keepdims=True))
        s_new = a * s_old + k_col * v_new
        s_ref[0, h] = s_new
        o = jnp.sum(q_col * s_new, axis=0, keepdims=True)
        on = o * lax.rsqrt(jnp.mean(o * o, axis=-1, keepdims=True) + EPS) * onw_ref[...]
        ob_ref[0, :, sl] = (on * zb_ref[0, :, sl]).astype(BF16)


def _delta_sample(u, conv_buf, conv_w, gb, zb, o_norm_w, s0):
    nb, _, cdim = u.shape
    row = lambda wd: pl.BlockSpec((1, 1, wd), lambda b: (b, 0, 0))
    st = pl.BlockSpec((1, N_HEADS_B, DK_B, DV_B), lambda b: (b, 0, 0, 0))
    return pl.pallas_call(
        _dec_delta_kernel,
        grid=(nb,),
        in_specs=[row(cdim), pl.BlockSpec((1, CONV_W - 1, cdim), lambda b: (b, 0, 0)),
                  pl.BlockSpec((CONV_W, cdim), lambda b: (0, 0)), row(LANES), row(B_W),
                  pl.BlockSpec((1, LANES), lambda b: (0, 0)), st],
        out_specs=[row(B_W), st],
        out_shape=[jax.ShapeDtypeStruct((nb, 1, B_W), BF16),
                   jax.ShapeDtypeStruct(s0.shape, F32)],
        compiler_params=_cparams(("parallel",)),
        name="delta_sample",
    )(u, conv_buf, conv_w, gb, zb, o_norm_w.reshape(1, LANES), s0)


def _out_kernel(oa_ref, za_ref, ob_ref, ga_ref, gbt_ref, x_ref, gate_ref, wa_ref, wb_ref, wo_ref, y_ref):
    ya = _dot((oa_ref[...] * za_ref[...]).astype(BF16), wa_ref[...])
    yb = _dot(ob_ref[...], wb_ref[...])
    mrg = ga_ref[...] * ya + gbt_ref[...] * yb
    y_ref[0] = x_ref[0] + gate_ref[0] * _dot(mrg.astype(BF16), wo_ref[...])


def _out_proj(oa, za, ob, gates, x, gate, wa, wb, wo, tm):
    b, s, d = x.shape
    nt = s // tm
    ts = tm if gate.shape[1] == s else 1
    gate_map = (lambda i, j: (i, j, 0)) if ts == tm else (lambda i, j: (i, 0, 0))
    rows = lambda wd, cb=0: pl.BlockSpec((tm, wd), lambda i, j, cb=cb: (i * nt + j, cb))
    const = lambda shp: pl.BlockSpec(shp, lambda i, j: (0, 0), pipeline_mode=pl.Buffered(1))
    return pl.pallas_call(
        _out_kernel,
        grid=(b, nt),
        in_specs=[rows(A_W), rows(A_W), rows(B_W), rows(d, 0), rows(d, 1),
                  pl.BlockSpec((1, tm, d), lambda i, j: (i, j, 0)),
                  pl.BlockSpec((1, ts, d), gate_map),
                  const(wa.shape), const(wb.shape), const(wo.shape)],
        out_specs=pl.BlockSpec((1, tm, d), lambda i, j: (i, j, 0)),
        out_shape=jax.ShapeDtypeStruct((b, s, d), F32),
        compiler_params=_cparams(("parallel", "parallel")),
        name="out_proj",
    )(oa, za, ob, gates, gates, x, gate, wa, wb, wo)


def _tile(n, pref):
    t = min(n, pref)
    while n % t:
        t //= 2
    return t


def _project(h2, wts, tm, kt_dims=None):
    (w_main, w_kt, w_g, cdim, gmat, qnw, knw, alog, dtb) = wts
    tn = A_W
    seg = lambda col, n: (w_main, col // tn, n)
    (q16,) = _proj("q", h2, seg(0, A_W), [gmat, qnw], [BF16], tm, tn)
    if kt_dims is None:
        k32, k16 = _proj("k", h2, seg(A_W, A_W), [gmat, knw], [F32, BF16], tm, tn)
    else:
        knw_col = knw[0, :HEAD_DIM_A].reshape(1, HEAD_DIM_A, 1)
        k32, k16 = _proj_kt(h2, w_kt, knw_col, kt_dims[0], kt_dims[1], tm)
    v32, v16 = _proj("v", h2, seg(2 * A_W, A_W), [], [F32, BF16], tm, tn)
    (za,) = _proj("silu", h2, seg(3 * A_W, A_W), [], [F32], tm, tn)
    (raw,) = _proj("raw", h2, seg(4 * A_W, cdim), [], [F32], tm, tn)
    (zb,) = _proj("silu", h2, seg(4 * A_W + cdim, B_W), [], [F32], tm, tn)
    (gb,) = _proj("ab", h2, (w_main, (4 * A_W + cdim + B_W) // LANES, LANES), [alog, dtb], [F32], tm, LANES)
    (gates,) = _proj("sigmoid", h2, (w_g, 0, w_g.shape[1]), [], [F32], tm, tn)
    return q16, k32, k16, v32, v16, za, raw, zb, gb, gates


def kernel(x_prompt, x_sample, c_prompt, c_sample, cache_k, cache_v, page_table, state_ssm, state_conv, w_ada, b_ada, norm_w, w_in, q_norm_w, k_norm_w, lam_q1, lam_k1, lam_q2, lam_k2, subln_w, conv_w, a_log, dt_bias, o_norm_w, w_branch_a, w_branch_b, w_out):
    depth = w_ada.shape[0]
    bsz, s_len, d = x_prompt.shape
    nb, t_dec, _ = x_sample.shape
    assert t_dec == 1 and s_len % CHUNK == 0 and s_len >= CONV_W - 1
    m_p = bsz * s_len
    cdim = conv_w.shape[-1]
    o_k, o_v = A_W, 2 * A_W
    o_ab = 4 * A_W + cdim + B_W
    o_g = o_ab + 2 * N_HEADS_B
    assert cdim % A_W == 0 and o_ab + LANES <= w_in.shape[-1]

    gi = jnp.arange(LANES) // HEAD_DIM_A
    gmat = jnp.where(gi[:, None] == gi[None, :], 1.0 / HEAD_DIM_A, 0.0).astype(BF16)

    yp, ys = x_prompt, x_sample
    outs = [[] for _ in range(8)]
    for li in range(depth):
        lambda_init = 0.8 - 0.6 * math.exp(-0.3 * li)
        wl = w_in[li]
        w_main = wl
        w_g = wl[:, o_g:o_g + 2 * d]
        w_kt = wl[:, o_k:o_v].T.astype(BF16)
        pad8 = lambda vec: jnp.pad(vec, (0, LANES - N_HEADS_B)).reshape(1, LANES)
        wts = (w_main, w_kt, w_g, cdim, gmat,
               jnp.tile(q_norm_w[li], 2 * N_HEADS_A).reshape(1, A_W),
               jnp.tile(k_norm_w[li], 2 * N_HEADS_A).reshape(1, A_W),
               pad8(a_log[li]), pad8(dt_bias[li]))
        wa16 = w_branch_a[li].astype(BF16)
        wbb16 = w_branch_b[li].astype(BF16)
        wo16 = w_out[li].astype(BF16)
        lamp = jnp.stack([lam_q1[li], lam_k1[li], lam_q2[li], lam_k2[li]])

        n_c = bsz + nb
        c_all = jnp.pad(jnp.concatenate([c_prompt, c_sample], axis=0), ((0, (-n_c) % 16), (0, 0)))
        mod = _ada(c_all, w_ada[li], b_ada[li])
        shift, scale, gate = mod[:, :d], mod[:, d:2 * d], mod[:, 2 * d:]

        tm = _tile(s_len, 1024)
        h = _hnorm(yp, shift[:bsz, None], scale[:bsz, None], norm_w[li], tm)
        q16, kt32, kt16, v32, v16, za, raw, zb, gb, gates = _project(h.reshape(m_p, d), wts, tm, (bsz, s_len))
        xs = ys.reshape(1, nb, d)
        sl_s = slice(bsz, bsz + nb)
        h_s = _hnorm(xs, shift[None, sl_s], scale[None, sl_s], norm_w[li], nb)
        q16_s, k32_s, _, v32_s, _, za_s, raw_s, zb_s, gb_s, gates_s = _project(h_s.reshape(nb, d), wts, nb)

        oa = _attn_prompt(q16, kt16, v16, lamp, subln_w[li], bsz, s_len, _tile(s_len, 512), lambda_init)
        n_pool = cache_k.shape[1]
        cache_kt = jnp.transpose(cache_k[li], (0, 2, 3, 4, 1)).reshape(n_pool, A_W, PAGE_SIZE)
        dec_args = (q16_s.astype(F32).reshape(nb, 1, A_W), k32_s.reshape(nb, 1, A_W), v32_s.reshape(nb, 1, A_W),
                    cache_kt, cache_v[li], page_table, lamp, subln_w[li], lambda_init)
        n_pages = page_table.shape[1]
        n_steps = nb * (n_pages // (PAGES_PER_STEP if n_pages % PAGES_PER_STEP == 0 else 1))
        hpu = _delta_units(bsz, s_len, n_steps)
        if hpu is None:
            oa_s = _attn_sample(*dec_args)
            ob, ssm_p = _delta_prompt(raw, conv_w[li], gb, zb, o_norm_w[li], bsz, s_len)
        else:
            oa_s, ob, ssm_p = _attn_sample(
                *dec_args, delta_args=(raw, conv_w[li], gb, zb, o_norm_w[li], bsz, s_len, hpu))
        ob_s, ssm_s = _delta_sample(raw_s.reshape(nb, 1, cdim), state_conv[li], conv_w[li],
                                    gb_s.reshape(nb, 1, LANES), zb_s.reshape(nb, 1, B_W), o_norm_w[li],
                                    state_ssm[li])

        yp = _out_proj(oa, za, ob, gates, yp, gate[:bsz, None], wa16, wbb16, wo16, _tile(s_len, 256))
        ys = _out_proj(oa_s.reshape(nb, A_W), za_s, ob_s.reshape(nb, B_W), gates_s, xs, gate[None, sl_s],
                       wa16, wbb16, wo16, nb).reshape(nb, 1, d)
        outs[0].append(jnp.transpose(kt32.reshape(bsz, N_HEADS_A, 2, HEAD_DIM_A, s_len), (0, 4, 1, 2, 3)))
        outs[1].append(v32.reshape(bsz, s_len, N_HEADS_A, 2 * HEAD_DIM_A))
        outs[2].append(k32_s.reshape(nb, 1, N_HEADS_A, 2, HEAD_DIM_A))
        outs[3].append(v32_s.reshape(nb, 1, N_HEADS_A, 2 * HEAD_DIM_A))
        outs[4].append(ssm_p)
        outs[5].append(ssm_s)
        outs[6].append(raw.reshape(bsz, s_len, cdim)[:, s_len - (CONV_W - 1):])
        outs[7].append(jnp.concatenate([state_conv[li][:, 1:], raw_s.reshape(nb, 1, cdim)], axis=1))

    st = [jnp.stack(o) for o in outs]
    return (yp, ys, st[0], st[1], st[2], st[3], st[4], st[5], st[6], st[7])
```

```python
import functools
import math

import jax
import jax.numpy as jnp
from jax import lax
from jax.experimental import pallas as pl
from jax.experimental.pallas import tpu as pltpu

F32 = jnp.float32
BF16 = jnp.bfloat16

N_HEADS_A = 8
HEAD_DIM_A = 64
A_W = N_HEADS_A * 2 * HEAD_DIM_A
N_HEADS_B = 8
DK_B = 128
DV_B = 128
B_W = N_HEADS_B * DK_B
CONV_W = 4
CHUNK = 64
PAGE_SIZE = 128
EPS = 1e-6
LANES = 128
SUBLANES = 8
VMEM_LIMIT = 56 * 1024 * 1024
PAGES_PER_STEP = 16
PROJ_SUBTILES = 4

_NT = (((1,), (1,)), ((), ()))
_TN = (((0,), (0,)), ((), ()))


def _cparams(sem):
    return pltpu.CompilerParams(dimension_semantics=sem, vmem_limit_bytes=VMEM_LIMIT)


def _dot(a, b):
    return jnp.dot(a, b, preferred_element_type=F32)


def _dot_nt(a, b):
    return lax.dot_general(a, b, _NT, preferred_element_type=F32)


def _dot_tn(a, b):
    return lax.dot_general(a, b, _TN, preferred_element_type=F32)


def _sigmoid(x):
    return 1.0 / (1.0 + jnp.exp(-x))


def _silu(x):
    return x * _sigmoid(x)


def _ada_kernel(c_ref, w_ref, b_ref, o_ref):
    sc = _silu(c_ref[...]).astype(BF16)
    o_ref[...] = _dot(sc, w_ref[...].astype(BF16)) + b_ref[...]


def _ada(c, w_ada, b_ada):
    m, d = c.shape
    n = w_ada.shape[1]
    tn = 512
    return pl.pallas_call(
        _ada_kernel,
        grid=(n // tn,),
        in_specs=[pl.BlockSpec((m, d), lambda j: (0, 0)),
                  pl.BlockSpec((d, tn), lambda j: (0, j)),
                  pl.BlockSpec((1, tn), lambda j: (0, j))],
        out_specs=pl.BlockSpec((m, tn), lambda j: (0, j)),
        out_shape=jax.ShapeDtypeStruct((m, n), F32),
        compiler_params=_cparams(("parallel",)),
        name="ada",
    )(c, w_ada, b_ada.reshape(1, n))


def _hnorm_kernel(x_ref, shift_ref, scale_ref, nw_ref, h_ref):
    x = x_ref[0]
    ms = jnp.mean(x * x, axis=-1, keepdims=True)
    y = x * lax.rsqrt(ms + EPS) * nw_ref[...]
    h_ref[0] = (y * (1.0 + scale_ref[0]) + shift_ref[0]).astype(BF16)


def _hnorm(x, shift, scale, norm_w, tm):
    b, s, d = x.shape
    ts = tm if shift.shape[1] == s else 1
    mod_map = (lambda i, j: (i, j, 0)) if ts == tm else (lambda i, j: (i, 0, 0))
    return pl.pallas_call(
        _hnorm_kernel,
        grid=(b, s // tm),
        in_specs=[pl.BlockSpec((1, tm, d), lambda i, j: (i, j, 0)),
                  pl.BlockSpec((1, ts, d), mod_map),
                  pl.BlockSpec((1, ts, d), mod_map),
                  pl.BlockSpec((1, d), lambda i, j: (0, 0))],
        out_specs=pl.BlockSpec((1, tm, d), lambda i, j: (i, j, 0)),
        out_shape=jax.ShapeDtypeStruct((b, s, d), BF16),
        compiler_params=_cparams(("parallel", "parallel")),
        name="hnorm",
    )(x, shift, scale, norm_w.reshape(1, d))


def _group_rms(acc, g_ref):
    sq = acc * acc
    hi = sq.astype(BF16)
    lo = (sq - hi.astype(F32)).astype(BF16)
    g = g_ref[...]
    outs = []
    for hh in range(acc.shape[1] // LANES):
        sl = slice(hh * LANES, (hh + 1) * LANES)
        ms = _dot(hi[:, sl], g) + _dot(lo[:, sl], g)
        outs.append(acc[:, sl] * lax.rsqrt(ms + EPS))
    return outs


def _proj_kernel(kind, h_ref, w_ref, *refs):
    *refs, w16_ref = refs

    @pl.when(pl.program_id(1) == 0)
    def _():
        w16_ref[...] = w_ref[...].astype(BF16)

    tm = h_ref.shape[0]
    nsub = PROJ_SUBTILES if tm % (PROJ_SUBTILES * LANES) == 0 else 1
    for r in range(nsub):
        rows = slice(r * (tm // nsub), (r + 1) * (tm // nsub))
        acc = _dot_nt(h_ref[rows, :], w16_ref[...])
        _proj_epilogue(kind, acc, rows, refs)


def _proj_epilogue(kind, acc, rows, refs):
    if kind == "q":
        g_ref, nw_ref, o_ref = refs
        for hh, y in enumerate(_group_rms(acc, g_ref)):
            sl = slice(hh * LANES, (hh + 1) * LANES)
            o_ref[rows, sl] = (y * nw_ref[:, sl] * (HEAD_DIM_A ** -0.5)).astype(BF16)
    elif kind == "k":
        g_ref, nw_ref, o_ref, ob_ref = refs
        for hh, y in enumerate(_group_rms(acc, g_ref)):
            sl = slice(hh * LANES, (hh + 1) * LANES)
            y = y * nw_ref[:, sl]
            o_ref[rows, sl] = y
            ob_ref[rows, sl] = y.astype(BF16)
    elif kind == "v":
        o_ref, ob_ref = refs
        o_ref[rows, :] = acc
        ob_ref[rows, :] = acc.astype(BF16)
    elif kind == "silu":
        (o_ref,) = refs
        o_ref[rows, :] = _silu(acc)
    elif kind == "sigmoid":
        (o_ref,) = refs
        o_ref[rows, :] = _sigmoid(acc)
    elif kind == "raw":
        (o_ref,) = refs
        o_ref[rows, :] = acc
    elif kind == "ab":
        alog_ref, dtb_ref, o_ref = refs
        a = acc + dtb_ref[...]
        sp = jnp.maximum(a, 0.0) + jnp.log(1.0 + jnp.exp(-jnp.abs(a)))
        g = -jnp.exp(alog_ref[...]) * sp
        lane = lax.broadcasted_iota(jnp.int32, acc.shape, 1)
        o_ref[rows, :] = jnp.where(lane < N_HEADS_B, g, _sigmoid(acc))
    else:
        raise ValueError(kind)


def _proj_kt_kernel(h_ref, wt_ref, nw_ref, o_ref, ob_ref, w16_ref):
    @pl.when(pl.program_id(0) == 0)
    def _():
        w16_ref[...] = wt_ref[...].astype(BF16)

    acc = _dot_nt(w16_ref[...], h_ref[...])
    tm = acc.shape[1]
    x3 = acc.reshape(2 * N_HEADS_A, HEAD_DIM_A, tm)
    ms = jnp.mean(x3 * x3, axis=1, keepdims=True)
    y = (x3 * lax.rsqrt(ms + EPS) * nw_ref[...]).reshape(A_W, tm)
    o_ref[0] = y
    ob_ref[0] = y.astype(BF16)


def _proj_kt(h, wseg, nw_col, bsz, s_len, tm):
    wt, row0 = wseg
    k = h.shape[1]
    nt = s_len // tm
    out_spec = pl.BlockSpec((1, A_W, tm), lambda i: (i // nt, 0, i % nt))
    return pl.pallas_call(
        _proj_kt_kernel,
        grid=(bsz * nt,),
        in_specs=[pl.BlockSpec((tm, k), lambda i: (i, 0)),
                  pl.BlockSpec((A_W, k), lambda i: (row0, 0)),
                  pl.BlockSpec(nw_col.shape, lambda i: (0, 0, 0))],
        out_specs=[out_spec, out_spec],
        out_shape=[jax.ShapeDtypeStruct((bsz, A_W, s_len), F32),
                   jax.ShapeDtypeStruct((bsz, A_W, s_len), BF16)],
        scratch_shapes=[pltpu.VMEM((A_W, k), BF16)],
        compiler_params=_cparams(("arbitrary",)),
        name="proj_kt",
    )(h, wt, nw_col)


def _proj(kind, h, wseg, extras, out_dtypes, tm, tn):
    w, row0, n = wseg
    m, k = h.shape
    extra_specs = [pl.BlockSpec(e.shape, lambda j, i, nd=e.ndim: (0,) * nd) for e in extras]
    if kind in ("q", "k"):
        extra_specs[1] = pl.BlockSpec((1, tn), lambda j, i: (0, j))
    outs = pl.pallas_call(
        functools.partial(_proj_kernel, kind),
        grid=(n // tn, m // tm),
        in_specs=[pl.BlockSpec((tm, k), lambda j, i: (i, 0)),
                  pl.BlockSpec((tn, k), lambda j, i: (row0 + j, 0))] + extra_specs,
        out_specs=[pl.BlockSpec((tm, tn), lambda j, i: (i, j)) for _ in out_dtypes],
        out_shape=[jax.ShapeDtypeStruct((m, n), dt) for dt in out_dtypes],
        scratch_shapes=[pltpu.VMEM((tn, k), BF16)],
        compiler_params=_cparams(("parallel", "arbitrary")),
        name="proj_" + kind,
    )(h, w, *extras)
    return outs


def _lam(lamp_ref, lambda_init):
    lp = lamp_ref[...]
    return (jnp.exp(jnp.sum(lp[0:1] * lp[1:2], axis=-1, keepdims=True))
            - jnp.exp(jnp.sum(lp[2:3] * lp[3:4], axis=-1, keepdims=True)) + lambda_init)


def _subln(o, sw, lambda_init):
    ms = jnp.mean(o * o, axis=-1, keepdims=True)
    return o * lax.rsqrt(ms + EPS) * sw * (1.0 - lambda_init)


def _attn_kernel(lamp_ref, q_ref, k_ref, v_ref, sw_ref, o_ref, m_sc, l_sc, acc_sc, *, tq, lambda_init):
    i = pl.program_id(2)
    q = q_ref[...]
    lane = lax.broadcasted_iota(jnp.int32, q.shape, 1)
    zero = jnp.zeros_like(q)
    qs = (jnp.where(lane < HEAD_DIM_A, q, zero), jnp.where(lane >= HEAD_DIM_A, q, zero))
    m_sc[...] = jnp.full(m_sc.shape, -jnp.inf, F32)
    l_sc[...] = jnp.zeros(l_sc.shape, F32)
    acc_sc[...] = jnp.zeros(acc_sc.shape, F32)

    def step(j, masked):
        start = pl.multiple_of(j * tq, tq)
        kb = k_ref[0, :, pl.ds(start, tq)]
        vb = v_ref[pl.ds(start, tq), :]
        for c in range(2):
            s = _dot(qs[c], kb)
            if masked:
                row = lax.broadcasted_iota(jnp.int32, s.shape, 0)
                col = lax.broadcasted_iota(jnp.int32, s.shape, 1)
                s = jnp.where(col <= row, s, -jnp.inf)
            m_prev = m_sc[c]
            m_new = jnp.maximum(m_prev, jnp.max(s, axis=-1, keepdims=True))
            alpha = jnp.exp(m_prev - m_new)
            ps = [jnp.exp(s[:, g * LANES:(g + 1) * LANES] - m_new) for g in range(tq // LANES)]
            l_sc[c] = alpha * l_sc[c] + functools.reduce(lambda a, b: a + b, ps)
            p = jnp.concatenate([x.astype(BF16) for x in ps], axis=1)
            acc_sc[c] = alpha * acc_sc[c] + _dot(p, vb)
            m_sc[c] = m_new

    def body(j, carry):
        step(j, False)
        return carry

    lax.fori_loop(0, i, body, 0)
    step(i, True)
    lam = _lam(lamp_ref, lambda_init)
    l0 = jnp.sum(l_sc[0], axis=-1, keepdims=True)
    l1 = jnp.sum(l_sc[1], axis=-1, keepdims=True)
    o = acc_sc[0] / l0 - lam * (acc_sc[1] / l1)
    o_ref[...] = _subln(o, sw_ref[...], lambda_init)


def _attn_prompt(q, k, v, lamp, subln_w, bsz, s_len, tq, lambda_init):
    m = q.shape[0]
    nq = s_len // tq
    return pl.pallas_call(
        functools.partial(_attn_kernel, tq=tq, lambda_init=lambda_init),
        grid=(bsz, N_HEADS_A, nq),
        in_specs=[pl.BlockSpec(lamp.shape, lambda b, h, i: (0, 0)),
                  pl.BlockSpec((tq, LANES), lambda b, h, i: (b * nq + i, h)),
                  pl.BlockSpec((1, LANES, s_len), lambda b, h, i: (b, h, 0)),
                  pl.BlockSpec((s_len, LANES), lambda b, h, i: (b, h)),
                  pl.BlockSpec((1, LANES), lambda b, h, i: (0, 0))],
        out_specs=pl.BlockSpec((tq, LANES), lambda b, h, i: (b * nq + i, h)),
        out_shape=jax.ShapeDtypeStruct((m, A_W), F32),
        scratch_shapes=[pltpu.VMEM((2, tq, LANES), F32)] * 3,
        compiler_params=_cparams(("parallel", "parallel", "parallel")),
        name="attn_prompt",
    )(lamp, q, k, v, subln_w.reshape(1, LANES))


def _decode_stages(qbd, e_ref, k_refs, v_refs, state, res):
    m_prev, l_prev, acc_prev = state
    parts = []
    pair = 2 if len(k_refs) % 2 == 0 else 1
    for p_i in range(0, len(k_refs), pair):
        kt = jnp.concatenate([k_refs[p_i + x][0].astype(BF16) for x in range(pair)], axis=1)
        parts.append(_dot(qbd, kt))
        yield
    s = jnp.concatenate(parts, axis=1)
    m_new = jnp.maximum(m_prev, jnp.max(s, axis=-1, keepdims=True))
    alpha = jnp.exp(m_prev - m_new)
    p = jnp.exp(s - m_new)
    res["m"] = m_new
    res["l"] = alpha * l_prev + jnp.sum(p, axis=-1, keepdims=True)
    nrow = p.shape[0]
    p_rows = jnp.concatenate([p[:, i * PAGE_SIZE:(i + 1) * PAGE_SIZE] for i in range(len(v_refs))], axis=0)
    pe_all = _dot(p_rows.astype(BF16), e_ref[...])
    yield
    pv = jnp.zeros(acc_prev.shape, F32)
    r_i = lax.broadcasted_iota(jnp.int32, (nrow, PAGE_SIZE * N_HEADS_A), 0)
    c_i = lax.broadcasted_iota(jnp.int32, (nrow, PAGE_SIZE * N_HEADS_A), 1)
    own = c_i % N_HEADS_A == r_i // 2
    for p_i, v_ref in enumerate(v_refs):
        pe = jnp.where(own, pe_all[p_i * nrow:(p_i + 1) * nrow], 0.0).astype(BF16)
        v2 = v_ref[0].reshape(PAGE_SIZE * N_HEADS_A, LANES).astype(BF16)
        pv = pv + _dot(pe, v2)
        yield
    res["acc"] = alpha * acc_prev + pv


def _dec_attn_kernel(pt_ref, lamp_ref, q_ref, kn_ref, vn_ref, sw_ref, e_ref, *refs, n_pages_step, lambda_init,
                     delta):
    k_refs = refs[:n_pages_step]
    v_refs = refs[n_pages_step:2 * n_pages_step]
    rest = refs[2 * n_pages_step:]
    if delta is None:
        o_ref, m_sc, l_sc, acc_sc = rest
    else:
        d_in = rest[:N_DELTA_IN]
        gb_ref, zb_ref, onw_ref = d_in[9:]
        o_ref, ob_ref, s_ref, m_sc, l_sc, acc_sc, ext_ref = rest[N_DELTA_IN:]
    j = pl.program_id(1)
    nrow = 2 * N_HEADS_A
    q = q_ref[0]
    hc = lax.broadcasted_iota(jnp.int32, (nrow, A_W), 0)
    lane = lax.broadcasted_iota(jnp.int32, (nrow, A_W), 1)
    qbd = jnp.where(lane // HEAD_DIM_A == hc, jnp.broadcast_to(q, (nrow, A_W)), 0.0).astype(BF16)

    @pl.when(j == 0)
    def _():
        kn = jnp.broadcast_to(kn_ref[0], (nrow, A_W)).astype(BF16).astype(F32)
        m_sc[...] = jnp.sum(qbd.astype(F32) * kn, axis=-1, keepdims=True)
        l_sc[...] = jnp.ones(l_sc.shape, F32)
        vn = vn_ref[0].astype(BF16).astype(F32)
        row_head = lax.broadcasted_iota(jnp.int32, (nrow, LANES), 0) // 2
        acc = jnp.zeros((nrow, LANES), F32)
        for h in range(N_HEADS_A):
            acc = jnp.where(row_head == h, jnp.broadcast_to(vn[:, h * LANES:(h + 1) * LANES], (nrow, LANES)), acc)
        acc_sc[...] = acc

    res = {}
    gens = [_decode_stages(qbd, e_ref, k_refs, v_refs, (m_sc[...], l_sc[...], acc_sc[...]), res)]
    if delta is not None:
        hpu, ng, nc, n_units = delta
        t = pl.program_id(0) * pl.num_programs(1) + j
        unit = jnp.minimum(t, n_units - 1)
        h0 = pl.multiple_of((unit % ng) * hpu, hpu)

        first_chunk = (unit // ng) % nc == 0

        @pl.when(first_chunk & (t < n_units))
        def _():
            s_ref[0, pl.ds(h0, hpu)] = jnp.zeros((hpu, DK_B, DV_B), F32)

        dres = {}

        def conv_then_delta():
            dq, dk, dv = _conv_unit(d_in[0:3], d_in[3:6], d_in[6:9], ext_ref, first_chunk, hpu)
            yield
            yield from _delta_stages(dq, dk, dv, gb_ref[0], s_ref[0, pl.ds(h0, hpu)], hpu, dres)

        gens.append(conv_then_delta())
    _interleave(*gens)
    m_sc[...] = res["m"]
    l_sc[...] = res["l"]
    acc_sc[...] = res["acc"]
    if delta is not None:
        slabs = _delta_finish(dres["o"], zb_ref, onw_ref, hpu)

        @pl.when(t < n_units)
        def _():
            s_ref[0, pl.ds(h0, hpu)] = dres["s"]
            for h, slab in enumerate(slabs):
                ob_ref[:, h * LANES:(h + 1) * LANES] = slab

    @pl.when(j == pl.num_programs(1) - 1)
    def _():
        lam = _lam(lamp_ref, lambda_init)
        o = acc_sc[...] / l_sc[...]
        for h in range(N_HEADS_A):
            oh = o[2 * h:2 * h + 1] - lam * o[2 * h + 1:2 * h + 2]
            o_ref[0, :, h * LANES:(h + 1) * LANES] = _subln(oh, sw_ref[...], lambda_init)


def _delta_units(bsz, s_len, n_steps):
    nc = s_len // CHUNK
    for hpu in (1, 2, 4, 8):
        if bsz * nc * (N_HEADS_B // hpu) <= n_steps:
            return hpu
    return None


def _attn_sample(q, kn, vn, cache_kt, cache_v, page_table, lamp, subln_w, lambda_init, delta_args=None):
    nb, n_pages = page_table.shape
    pps = PAGES_PER_STEP if n_pages % PAGES_PER_STEP == 0 else 1
    nj = n_pages // pps
    row_spec = pl.BlockSpec((1, 1, A_W), lambda b, j, pt: (b, 0, 0))
    d_in_specs, d_out_specs, d_out_shapes, d_args, d_scratch, delta = [], [], [], [], [], None
    if delta_args is not None:
        raw, conv_w, gb, zb, o_norm_w, bsz, s_len, hpu = delta_args
        nc = s_len // CHUNK
        ng = N_HEADS_B // hpu
        n_units = bsz * nc * ng
        delta = (hpu, ng, nc, n_units)
        unit = lambda b, j: jnp.minimum(b * nj + j, n_units - 1)
        d_in_specs, d_args, ob_spec, ext = _delta_operands(
            raw, conv_w, gb, zb, o_norm_w, hpu, lambda b, j, pt: (unit(b, j) // ng, unit(b, j) % ng))
        d_out_specs = [ob_spec, pl.BlockSpec((1, N_HEADS_B, DK_B, DV_B),
                                             lambda b, j, pt: (unit(b, j) // (ng * nc), 0, 0, 0))]
        d_out_shapes = [jax.ShapeDtypeStruct((raw.shape[0], B_W), BF16),
                        jax.ShapeDtypeStruct((bsz, N_HEADS_B, DK_B, DV_B), F32)]
        d_scratch = [ext]

    def page_spec(p_i, shape):
        return pl.BlockSpec((1,) + shape, lambda b, j, pt: (pt[b, j * pps + p_i],) + (0,) * len(shape))

    expand = (jnp.arange(PAGE_SIZE)[:, None] == jnp.arange(PAGE_SIZE * N_HEADS_A)[None, :] // N_HEADS_A).astype(BF16)
    grid_spec = pltpu.PrefetchScalarGridSpec(
        num_scalar_prefetch=1,
        grid=(nb, nj),
        in_specs=[pl.BlockSpec(lamp.shape, lambda b, j, pt: (0, 0)), row_spec, row_spec, row_spec,
                  pl.BlockSpec((1, LANES), lambda b, j, pt: (0, 0)),
                  pl.BlockSpec(expand.shape, lambda b, j, pt: (0, 0))]
                 + [page_spec(p_i, (A_W, PAGE_SIZE)) for p_i in range(pps)]
                 + [page_spec(p_i, (PAGE_SIZE, N_HEADS_A, LANES)) for p_i in range(pps)]
                 + d_in_specs,
        out_specs=[row_spec] + d_out_specs,
        scratch_shapes=[pltpu.VMEM((2 * N_HEADS_A, 1), F32), pltpu.VMEM((2 * N_HEADS_A, 1), F32),
                        pltpu.VMEM((2 * N_HEADS_A, LANES), F32)] + d_scratch,
    )
    outs = pl.pallas_call(
        functools.partial(_dec_attn_kernel, n_pages_step=pps, lambda_init=lambda_init, delta=delta),
        grid_spec=grid_spec,
        out_shape=[jax.ShapeDtypeStruct((nb, 1, A_W), F32)] + d_out_shapes,
        compiler_params=_cparams(("arbitrary", "arbitrary")),
        name="attn_sample",
    )(page_table, lamp, q, kn, vn, subln_w.reshape(1, LANES), expand,
      *([cache_kt] * pps), *([cache_v] * pps), *d_args)
    return outs if delta_args is not None else outs[0]


def _l2n(y):
    return y * lax.rsqrt(jnp.sum(y * y, axis=-1, keepdims=True) + EPS)


def _conv_unit(raw_refs, prev_refs, w_refs, ext_ref, first_chunk, nh):
    hist = SUBLANES - (CONV_W - 1)
    outs = []
    for s in range(3):
        ext_ref[s, 0:SUBLANES, :] = jnp.where(first_chunk, 0.0, prev_refs[s][...])
        ext_ref[s, SUBLANES:SUBLANES + CHUNK, :] = raw_refs[s][...]
        w = w_refs[s][...]
        y = ext_ref[s, hist:hist + CHUNK, :] * w[0:1]
        for jj in range(1, CONV_W):
            y = y + ext_ref[s, hist + jj:hist + jj + CHUNK, :] * w[jj:jj + 1]
        y = _silu(y)
        slabs = [y[:, h * LANES:(h + 1) * LANES] for h in range(nh)]
        if s == 0:
            slabs = [_l2n(x) * (DK_B ** -0.5) for x in slabs]
        elif s == 1:
            slabs = [_l2n(x) for x in slabs]
        outs.append(jnp.stack(slabs))
    return outs


def _bdot(a, b):
    return lax.dot_general(a, b, (((2,), (1,)), ((0,), (0,))), preferred_element_type=F32)


def _bdot_nt(a, b):
    return lax.dot_general(a, b, (((2,), (2,)), ((0,), (0,))), preferred_element_type=F32)


def _bdot_tn(a, b):
    return lax.dot_general(a, b, (((1,), (1,)), ((0,), (0,))), preferred_element_type=F32)


def _split_bdot(a, b):
    ah = a.astype(BF16)
    al = (a - ah.astype(F32)).astype(BF16)
    bh = b.astype(BF16)
    bl = (b - bh.astype(F32)).astype(BF16)
    return _bdot(ah, bh) + _bdot(ah, bl) + _bdot(al, bh)


def _cumsum_rows(x):
    n = x.shape[0]
    row = lax.broadcasted_iota(jnp.int32, x.shape, 0)
    s = 1
    while s < n:
        x = x + jnp.where(row >= s, pltpu.roll(x, s, axis=0), 0.0)
        s *= 2
    return x


def _delta_stages(q, k, v, gb, s_old, nh, res):
    lane_bcast = lambda x, off: jnp.stack(
        [jnp.broadcast_to(x[:, off + h:off + h + 1], (CHUNK, LANES)) for h in range(nh)])
    gc_all = _cumsum_rows(gb)
    gc_t = gc_all.T
    gc = lane_bcast(gc_all, 0)
    gc_row = jnp.stack([gc_t[h:h + 1, :] for h in range(nh)])
    beta = lane_bcast(gb, nh)
    row = lax.broadcasted_iota(jnp.int32, (1, CHUNK, CHUNK), 1)
    col = lax.broadcasted_iota(jnp.int32, (1, CHUNK, CHUNK), 2)
    tril = col <= row
    strict = col < row
    eye = jnp.where(row == col, 1.0, 0.0)
    decay = jnp.exp(jnp.where(tril, gc[:, :, :CHUNK] - gc_row, -jnp.inf))
    kb = k * beta
    k16 = k.astype(BF16)
    lmat = jnp.where(strict, _bdot_nt(kb.astype(BF16), k16) * decay, 0.0)
    qk = jnp.where(tril, _bdot_nt(q.astype(BF16), k16) * decay, 0.0)
    yield
    tinv = eye - lmat
    pw = lmat
    for _ in range(5):
        pw = _split_bdot(pw, pw)
        yield
        tinv = tinv + _split_bdot(tinv, pw)
        yield
    t16 = tinv.astype(BF16)
    u = _bdot(t16, (v * beta).astype(BF16))
    wk = _bdot(t16, (kb * jnp.exp(gc)).astype(BF16))
    s16 = s_old.astype(BF16)
    yield
    v_new = u - _bdot(wk.astype(BF16), s16)
    vn16 = v_new.astype(BF16)
    yield
    res["o"] = _bdot((q * jnp.exp(gc)).astype(BF16), s16) + _bdot(qk.astype(BF16), vn16)
    g_last = gc[:, CHUNK - 1:CHUNK, :]
    k_dec = k * jnp.exp(g_last - gc)
    res["s"] = s_old * jnp.exp(g_last) + _bdot_tn(k_dec.astype(BF16), vn16)


def _interleave(*gens):
    live = list(gens)
    while live:
        for g in list(live):
            try:
                next(g)
            except StopIteration:
                live.remove(g)


def _delta_finish(o, zb_ref, onw_ref, nh):
    on = o * lax.rsqrt(jnp.mean(o * o, axis=-1, keepdims=True) + EPS) * onw_ref[...]
    return [(on[h] * zb_ref[:, h * LANES:(h + 1) * LANES]).astype(BF16) for h in range(nh)]


N_DELTA_IN = 12


def _delta_kernel(*refs):
    d_in = refs[:N_DELTA_IN]
    ob_ref, s_ref, ext_ref = refs[N_DELTA_IN:]
    gb_ref, zb_ref, onw_ref = d_in[9:]
    c = pl.program_id(1)
    nh = N_HEADS_B

    @pl.when(c == 0)
    def _():
        s_ref[...] = jnp.zeros(s_ref.shape, F32)

    q, k, v = _conv_unit(d_in[0:3], d_in[3:6], d_in[6:9], ext_ref, c == 0, nh)
    res = {}
    _interleave(_delta_stages(q, k, v, gb_ref[0], s_ref[0], nh, res))
    s_ref[0] = res["s"]
    for h, slab in enumerate(_delta_finish(res["o"], zb_ref, onw_ref, nh)):
        ob_ref[:, h * LANES:(h + 1) * LANES] = slab


def _delta_operands(raw, conv_w, gb, zb, o_norm_w, hpu, where):
    ng = N_HEADS_B // hpu
    wd = hpu * LANES
    rc = lambda *a: where(*a)[0]
    hg = lambda *a: where(*a)[1]
    cur = lambda s: pl.BlockSpec((CHUNK, wd), lambda *a, s=s: (rc(*a), s * ng + hg(*a)))
    prev = lambda s: pl.BlockSpec(
        (SUBLANES, wd), lambda *a, s=s: (jnp.maximum(rc(*a) * (CHUNK // SUBLANES) - 1, 0), s * ng + hg(*a)))
    taps = lambda s: pl.BlockSpec((CONV_W, wd), lambda *a, s=s: (0, s * ng + hg(*a)))
    specs = ([cur(s) for s in range(3)] + [prev(s) for s in range(3)] + [taps(s) for s in range(3)]
             + [pl.BlockSpec((1, CHUNK, LANES), lambda *a: (hg(*a), rc(*a), 0)),
                pl.BlockSpec((CHUNK, wd), lambda *a: (rc(*a), hg(*a))),
                pl.BlockSpec((1, LANES), lambda *a: (0, 0))])
    args = [raw] * 6 + [conv_w] * 3 + [_group_gates(gb, hpu), zb, o_norm_w.reshape(1, LANES)]
    out_spec = pl.BlockSpec((CHUNK, wd), lambda *a: (rc(*a), hg(*a)))
    scratch = pltpu.VMEM((3, CHUNK + SUBLANES, wd), F32)
    return specs, args, out_spec, scratch


def _group_gates(gb, hpu):
    m = gb.shape[0]
    ng = N_HEADS_B // hpu
    g = gb[:, :N_HEADS_B].reshape(m, ng, hpu)
    beta = gb[:, N_HEADS_B:2 * N_HEADS_B].reshape(m, ng, hpu)
    gbg = jnp.transpose(jnp.concatenate([g, beta], axis=-1), (1, 0, 2))
    return jnp.pad(gbg, ((0, 0), (0, 0), (0, LANES - 2 * hpu)))


def _delta_prompt(raw, conv_w, gb, zb, o_norm_w, bsz, s_len):
    m = raw.shape[0]
    nc = s_len // CHUNK
    specs, args, out_spec, scratch = _delta_operands(raw, conv_w, gb, zb, o_norm_w, N_HEADS_B,
                                                     lambda b, c: (b * nc + c, 0))
    return pl.pallas_call(
        _delta_kernel,
        grid=(bsz, nc),
        in_specs=specs,
        out_specs=[out_spec, pl.BlockSpec((1, N_HEADS_B, DK_B, DV_B), lambda b, c: (b, 0, 0, 0))],
        out_shape=[jax.ShapeDtypeStruct((m, B_W), BF16),
                   jax.ShapeDtypeStruct((bsz, N_HEADS_B, DK_B, DV_B), F32)],
        scratch_shapes=[scratch],
        compiler_params=_cparams(("parallel", "arbitrary")),
        name="delta_prompt",
    )(*args)


def _dec_delta_kernel(u_ref, buf_ref, w_ref, gb_ref, zb_ref, onw_ref, s0_ref, ob_ref, s_ref):
    w = w_ref[...]
    buf = buf_ref[0]
    y = buf[0:1] * w[0:1] + buf[1:2] * w[1:2] + buf[2:3] * w[2:3] + u_ref[0] * w[3:4]
    y = _silu(y)
    gb = gb_ref[0]
    r = lax.broadcasted_iota(jnp.int32, (DK_B, LANES), 0)
    cidx = lax.broadcasted_iota(jnp.int32, (DK_B, LANES), 1)
    eye = r == cidx

    def column(x):
        return jnp.sum(jnp.where(eye, jnp.broadcast_to(x, (DK_B, LANES)), 0.0), axis=-1, keepdims=True)

    for h in range(N_HEADS_B):
        sl = slice(h * LANES, (h + 1) * LANES)
        q = _l2n(y[:, h * LANES:(h + 1) * LANES]) * (DK_B ** -0.5)
        k = _l2n(y[:, B_W + h * LANES:B_W + (h + 1) * LANES])
        v = y[:, 2 * B_W + h * LANES:2 * B_W + (h + 1) * LANES]
        a = jnp.exp(gb[:, h:h + 1])
        beta = gb[:, N_HEADS_B + h:N_HEADS_B + h + 1]
        s_old = s0_ref[0, h]
        k_col = column(k)
        q_col = column(q)
        v_new = beta * (v - a * jnp.sum(k_col * s_old, axis=0, keepdims=True))
        s_new = a * s_old + k_col * v_new
        s_ref[0, h] = s_new
        o = jnp.sum(q_col * s_new, axis=0, keepdims=True)
        on = o * lax.rsqrt(jnp.mean(o * o, axis=-1, keepdims=True) + EPS) * onw_ref[...]
        ob_ref[0, :, sl] = (on * zb_ref[0, :, sl]).astype(BF16)


def _delta_sample(u, conv_buf, conv_w, gb, zb, o_norm_w, s0):
    nb, _, cdim = u.shape
    row = lambda wd: pl.BlockSpec((1, 1, wd), lambda b: (b, 0, 0))
    st = pl.BlockSpec((1, N_HEADS_B, DK_B, DV_B), lambda b: (b, 0, 0, 0))
    return pl.pallas_call(
        _dec_delta_kernel,
        grid=(nb,),
        in_specs=[row(cdim), pl.BlockSpec((1, CONV_W - 1, cdim), lambda b: (b, 0, 0)),
                  pl.BlockSpec((CONV_W, cdim), lambda b: (0, 0)), row(LANES), row(B_W),
                  pl.BlockSpec((1, LANES), lambda b: (0, 0)), st],
        out_specs=[row(B_W), st],
        out_shape=[jax.ShapeDtypeStruct((nb, 1, B_W), BF16),
                   jax.ShapeDtypeStruct(s0.shape, F32)],
        compiler_params=_cparams(("parallel",)),
        name="delta_sample",
    )(u, conv_buf, conv_w, gb, zb, o_norm_w.reshape(1, LANES), s0)


def _out_kernel(oa_ref, za_ref, ob_ref, ga_ref, gbt_ref, x_ref, gate_ref, wa_ref, wb_ref, wo_ref, y_ref):
    ya = _dot((oa_ref[...] * za_ref[...]).astype(BF16), wa_ref[...])
    yb = _dot(ob_ref[...], wb_ref[...])
    mrg = ga_ref[...] * ya + gbt_ref[...] * yb
    y_ref[0] = x_ref[0] + gate_ref[0] * _dot(mrg.astype(BF16), wo_ref[...])


def _out_proj(oa, za, ob, gates, x, gate, wa, wb, wo, tm):
    b, s, d = x.shape
    nt = s // tm
    ts = tm if gate.shape[1] == s else 1
    gate_map = (lambda i, j: (i, j, 0)) if ts == tm else (lambda i, j: (i, 0, 0))
    rows = lambda wd, cb=0: pl.BlockSpec((tm, wd), lambda i, j, cb=cb: (i * nt + j, cb))
    const = lambda shp: pl.BlockSpec(shp, lambda i, j: (0, 0), pipeline_mode=pl.Buffered(1))
    return pl.pallas_call(
        _out_kernel,
        grid=(b, nt),
        in_specs=[rows(A_W), rows(A_W), rows(B_W), rows(d, 0), rows(d, 1),
                  pl.BlockSpec((1, tm, d), lambda i, j: (i, j, 0)),
                  pl.BlockSpec((1, ts, d), gate_map),
                  const(wa.shape), const(wb.shape), const(wo.shape)],
        out_specs=pl.BlockSpec((1, tm, d), lambda i, j: (i, j, 0)),
        out_shape=jax.ShapeDtypeStruct((b, s, d), F32),
        compiler_params=_cparams(("parallel", "parallel")),
        name="out_proj",
    )(oa, za, ob, gates, gates, x, gate, wa, wb, wo)


def _tile(n, pref):
    t = min(n, pref)
    while n % t:
        t //= 2
    return t


def _project(h2, wts, tm, kt_dims=None):
    (w_main, w_g, cdim, gmat, qnw, knw, alog, dtb) = wts
    tn = A_W
    seg = lambda col, n: (w_main, col // tn, n)
    (q16,) = _proj("q", h2, seg(0, A_W), [gmat, qnw], [BF16], tm, tn)
    if kt_dims is None:
        k32, k16 = _proj("k", h2, seg(A_W, A_W), [gmat, knw], [F32, BF16], tm, tn)
    else:
        knw_col = knw[0, :HEAD_DIM_A].reshape(1, HEAD_DIM_A, 1)
        k32, k16 = _proj_kt(h2, (w_main, 1), knw_col, kt_dims[0], kt_dims[1], tm)
    v32, v16 = _proj("v", h2, seg(2 * A_W, A_W), [], [F32, BF16], tm, tn)
    (za,) = _proj("silu", h2, seg(3 * A_W, A_W), [], [F32], tm, tn)
    (raw,) = _proj("raw", h2, seg(4 * A_W, cdim), [], [F32], tm, tn)
    (zb,) = _proj("silu", h2, seg(4 * A_W + cdim, B_W), [], [F32], tm, tn)
    (gb,) = _proj("ab", h2, (w_main, (4 * A_W + cdim + B_W) // LANES, LANES), [alog, dtb], [F32], tm, LANES)
    (gates,) = _proj("sigmoid", h2, (w_g, 0, w_g.shape[0]), [], [F32], tm, tn)
    return q16, k32, k16, v32, v16, za, raw, zb, gb, gates


def kernel(x_prompt, x_sample, c_prompt, c_sample, cache_k, cache_v, page_table, state_ssm, state_conv, w_ada, b_ada, norm_w, w_in, q_norm_w, k_norm_w, lam_q1, lam_k1, lam_q2, lam_k2, subln_w, conv_w, a_log, dt_bias, o_norm_w, w_branch_a, w_branch_b, w_out):
    depth = w_ada.shape[0]
    bsz, s_len, d = x_prompt.shape
    nb, t_dec, _ = x_sample.shape
    assert t_dec == 1 and s_len % CHUNK == 0 and s_len >= CONV_W - 1
    m_p = bsz * s_len
    cdim = conv_w.shape[-1]
    o_k, o_v = A_W, 2 * A_W
    o_ab = 4 * A_W + cdim + B_W
    o_g = o_ab + 2 * N_HEADS_B
    assert cdim % A_W == 0 and o_ab + LANES <= w_in.shape[-1]

    gi = jnp.arange(LANES) // HEAD_DIM_A
    gmat = jnp.where(gi[:, None] == gi[None, :], 1.0 / HEAD_DIM_A, 0.0).astype(BF16)

    yp, ys = x_prompt, x_sample
    outs = [[] for _ in range(8)]
    for li in range(depth):
        lambda_init = 0.8 - 0.6 * math.exp(-0.3 * li)
        w_main = w_in[li].T
        w_g = w_main[o_g:o_g + 2 * d]
        pad8 = lambda vec: jnp.pad(vec, (0, LANES - N_HEADS_B)).reshape(1, LANES)
        wts = (w_main, w_g, cdim, gmat,
               jnp.tile(q_norm_w[li], 2 * N_HEADS_A).reshape(1, A_W),
               jnp.tile(k_norm_w[li], 2 * N_HEADS_A).reshape(1, A_W),
               pad8(a_log[li]), pad8(dt_bias[li]))
        wa16 = w_branch_a[li].astype(BF16)
        wbb16 = w_branch_b[li].astype(BF16)
        wo16 = w_out[li].astype(BF16)
        lamp = jnp.stack([lam_q1[li], lam_k1[li], lam_q2[li], lam_k2[li]])

        n_c = bsz + nb
        c_all = jnp.pad(jnp.concatenate([c_prompt, c_sample], axis=0), ((0, (-n_c) % 16), (0, 0)))
        mod = _ada(c_all, w_ada[li], b_ada[li])
        shift, scale, gate = mod[:, :d], mod[:, d:2 * d], mod[:, 2 * d:]

        tm = _tile(s_len, 1024)
        h = _hnorm(yp, shift[:bsz, None], scale[:bsz, None], norm_w[li], tm)
        q16, kt32, kt16, v32, v16, za, raw, zb, gb, gates = _project(h.reshape(m_p, d), wts, tm, (bsz, s_len))
        xs = ys.reshape(1, nb, d)
        sl_s = slice(bsz, bsz + nb)
        h_s = _hnorm(xs, shift[None, sl_s], scale[None, sl_s], norm_w[li], nb)
        q16_s, k32_s, _, v32_s, _, za_s, raw_s, zb_s, gb_s, gates_s = _project(h_s.reshape(nb, d), wts, nb)

        oa = _attn_prompt(q16, kt16, v16, lamp, subln_w[li], bsz, s_len, _tile(s_len, 512), lambda_init)
        n_pool = cache_k.shape[1]
        cache_kt = jnp.transpose(cache_k[li], (0, 2, 3, 4, 1)).reshape(n_pool, A_W, PAGE_SIZE)
        dec_args = (q16_s.astype(F32).reshape(nb, 1, A_W), k32_s.reshape(nb, 1, A_W), v32_s.reshape(nb, 1, A_W),
                    cache_kt, cache_v[li], page_table, lamp, subln_w[li], lambda_init)
        n_pages = page_table.shape[1]
        n_steps = nb * (n_pages // (PAGES_PER_STEP if n_pages % PAGES_PER_STEP == 0 else 1))
        hpu = _delta_units(bsz, s_len, n_steps)
        if hpu is None:
            oa_s = _attn_sample(*dec_args)
            ob, ssm_p = _delta_prompt(raw, conv_w[li], gb, zb, o_norm_w[li], bsz, s_len)
        else:
            oa_s, ob, ssm_p = _attn_sample(
                *dec_args, delta_args=(raw, conv_w[li], gb, zb, o_norm_w[li], bsz, s_len, hpu))
        ob_s, ssm_s = _delta_sample(raw_s.reshape(nb, 1, cdim), state_conv[li], conv_w[li],
                                    gb_s.reshape(nb, 1, LANES), zb_s.reshape(nb, 1, B_W), o_norm_w[li],
                                    state_ssm[li])

        yp = _out_proj(oa, za, ob, gates, yp, gate[:bsz, None], wa16, wbb16, wo16, _tile(s_len, 256))
        ys = _out_proj(oa_s.reshape(nb, A_W), za_s, ob_s.reshape(nb, B_W), gates_s, xs, gate[None, sl_s],
                       wa16, wbb16, wo16, nb).reshape(nb, 1, d)
        outs[0].append(jnp.transpose(kt32.reshape(bsz, N_HEADS_A, 2, HEAD_DIM_A, s_len), (0, 4, 1, 2, 3)))
        outs[1].append(v32.reshape(bsz, s_len, N_HEADS_A, 2 * HEAD_DIM_A))
        outs[2].append(k32_s.reshape(nb, 1, N_HEADS_A, 2, HEAD_DIM_A))
        outs[3].append(v32_s.reshape(nb, 1, N_HEADS_A, 2 * HEAD_DIM_A))
        outs[4].append(ssm_p)
        outs[5].append(ssm_s)
        outs[6].append(raw.reshape(bsz, s_len, cdim)[:, s_len - (CONV_W - 1):])
        outs[7].append(jnp.concatenate([state_conv[li][:, 1:], raw_s.reshape(nb, 1, cdim)], axis=1))

    st = [jnp.stack(o) for o in outs]
    return (yp, ys, st[0], st[1], st[2], st[3], st[4], st[5], st[6], st[7])
```

```python
import functools
import math

import jax
import jax.numpy as jnp
from jax import lax
from jax.experimental import pallas as pl
from jax.experimental.pallas import tpu as pltpu

F32 = jnp.float32
BF16 = jnp.bfloat16

N_HEADS_A = 8
HEAD_DIM_A = 64
A_W = N_HEADS_A * 2 * HEAD_DIM_A
N_HEADS_B = 8
DK_B = 128
DV_B = 128
B_W = N_HEADS_B * DK_B
CONV_W = 4
CHUNK = 64
PAGE_SIZE = 128
EPS = 1e-6
LANES = 128
SUBLANES = 8
VMEM_LIMIT = 56 * 1024 * 1024
PAGES_PER_STEP = 16
PROJ_SUBTILES = 4

_NT = (((1,), (1,)), ((), ()))
_TN = (((0,), (0,)), ((), ()))


def _cparams(sem):
    return pltpu.CompilerParams(dimension_semantics=sem, vmem_limit_bytes=VMEM_LIMIT)


def _dot(a, b):
    return jnp.dot(a, b, preferred_element_type=F32)


def _dot_nt(a, b):
    return lax.dot_general(a, b, _NT, preferred_element_type=F32)


def _dot_tn(a, b):
    return lax.dot_general(a, b, _TN, preferred_element_type=F32)


def _sigmoid(x):
    return 1.0 / (1.0 + jnp.exp(-x))


def _silu(x):
    return x * _sigmoid(x)


def _ada_kernel(c_ref, w_ref, b_ref, o_ref):
    sc = _silu(c_ref[...]).astype(BF16)
    o_ref[...] = _dot(sc, w_ref[...].astype(BF16)) + b_ref[...]


def _ada(c, w_ada, b_ada):
    m, d = c.shape
    n = w_ada.shape[1]
    tn = 512
    return pl.pallas_call(
        _ada_kernel,
        grid=(n // tn,),
        in_specs=[pl.BlockSpec((m, d), lambda j: (0, 0)),
                  pl.BlockSpec((d, tn), lambda j: (0, j)),
                  pl.BlockSpec((1, tn), lambda j: (0, j))],
        out_specs=pl.BlockSpec((m, tn), lambda j: (0, j)),
        out_shape=jax.ShapeDtypeStruct((m, n), F32),
        compiler_params=_cparams(("parallel",)),
        name="ada",
    )(c, w_ada, b_ada.reshape(1, n))


def _hnorm_kernel(x_ref, shift_ref, scale_ref, nw_ref, h_ref):
    x = x_ref[0]
    ms = jnp.mean(x * x, axis=-1, keepdims=True)
    y = x * lax.rsqrt(ms + EPS) * nw_ref[...]
    h_ref[0] = (y * (1.0 + scale_ref[0]) + shift_ref[0]).astype(BF16)


def _hnorm(x, shift, scale, norm_w, tm):
    b, s, d = x.shape
    ts = tm if shift.shape[1] == s else 1
    mod_map = (lambda i, j: (i, j, 0)) if ts == tm else (lambda i, j: (i, 0, 0))
    return pl.pallas_call(
        _hnorm_kernel,
        grid=(b, s // tm),
        in_specs=[pl.BlockSpec((1, tm, d), lambda i, j: (i, j, 0)),
                  pl.BlockSpec((1, ts, d), mod_map),
                  pl.BlockSpec((1, ts, d), mod_map),
                  pl.BlockSpec((1, d), lambda i, j: (0, 0))],
        out_specs=pl.BlockSpec((1, tm, d), lambda i, j: (i, j, 0)),
        out_shape=jax.ShapeDtypeStruct((b, s, d), BF16),
        compiler_params=_cparams(("parallel", "parallel")),
        name="hnorm",
    )(x, shift, scale, norm_w.reshape(1, d))


def _group_rms(acc, g_ref):
    sq = acc * acc
    hi = sq.astype(BF16)
    lo = (sq - hi.astype(F32)).astype(BF16)
    g = g_ref[...]
    outs = []
    for hh in range(acc.shape[1] // LANES):
        sl = slice(hh * LANES, (hh + 1) * LANES)
        ms = _dot(hi[:, sl], g) + _dot(lo[:, sl], g)
        outs.append(acc[:, sl] * lax.rsqrt(ms + EPS))
    return outs


def _proj_kernel(kind, shift, h_ref, w_ref, *refs):
    *refs, w16_ref = refs
    tn = w16_ref.shape[0]
    if shift:
        w2_ref, *refs = refs

    @pl.when(pl.program_id(1) == 0)
    def _():
        if shift:
            w16_ref[0:tn - shift, :] = w_ref[shift:tn, :].astype(BF16)
            w16_ref[tn - shift:tn, :] = w2_ref[0:shift, :].astype(BF16)
        else:
            w16_ref[...] = w_ref[...].astype(BF16)

    tm = h_ref.shape[0]
    nsub = PROJ_SUBTILES if tm % (PROJ_SUBTILES * LANES) == 0 else 1
    for r in range(nsub):
        rows = slice(r * (tm // nsub), (r + 1) * (tm // nsub))
        acc = _dot_nt(h_ref[rows, :], w16_ref[...])
        _proj_epilogue(kind, acc, rows, refs)


def _proj_epilogue(kind, acc, rows, refs):
    if kind == "q":
        g_ref, nw_ref, o_ref = refs
        for hh, y in enumerate(_group_rms(acc, g_ref)):
            sl = slice(hh * LANES, (hh + 1) * LANES)
            o_ref[rows, sl] = (y * nw_ref[:, sl] * (HEAD_DIM_A ** -0.5)).astype(BF16)
    elif kind == "k":
        g_ref, nw_ref, o_ref, ob_ref = refs
        for hh, y in enumerate(_group_rms(acc, g_ref)):
            sl = slice(hh * LANES, (hh + 1) * LANES)
            y = y * nw_ref[:, sl]
            o_ref[rows, sl] = y
            ob_ref[rows, sl] = y.astype(BF16)
    elif kind == "v":
        o_ref, ob_ref = refs
        o_ref[rows, :] = acc
        ob_ref[rows, :] = acc.astype(BF16)
    elif kind == "silu":
        (o_ref,) = refs
        o_ref[rows, :] = _silu(acc)
    elif kind == "sigmoid":
        (o_ref,) = refs
        o_ref[rows, :] = _sigmoid(acc)
    elif kind == "raw":
        (o_ref,) = refs
        o_ref[rows, :] = acc
    elif kind == "ab":
        alog_ref, dtb_ref, o_ref = refs
        a = acc + dtb_ref[...]
        sp = jnp.maximum(a, 0.0) + jnp.log(1.0 + jnp.exp(-jnp.abs(a)))
        g = -jnp.exp(alog_ref[...]) * sp
        lane = lax.broadcasted_iota(jnp.int32, acc.shape, 1)
        o_ref[rows, :] = jnp.where(lane < N_HEADS_B, g, _sigmoid(acc))
    else:
        raise ValueError(kind)


def _proj_kt_kernel(h_ref, wt_ref, nw_ref, o_ref, ob_ref, w16_ref):
    @pl.when(pl.program_id(0) == 0)
    def _():
        w16_ref[...] = wt_ref[...].astype(BF16)

    acc = _dot_nt(w16_ref[...], h_ref[...])
    tm = acc.shape[1]
    x3 = acc.reshape(2 * N_HEADS_A, HEAD_DIM_A, tm)
    ms = jnp.mean(x3 * x3, axis=1, keepdims=True)
    y = (x3 * lax.rsqrt(ms + EPS) * nw_ref[...]).reshape(A_W, tm)
    o_ref[0] = y
    ob_ref[0] = y.astype(BF16)


def _proj_kt(h, wseg, nw_col, bsz, s_len, tm):
    wt, row0 = wseg
    k = h.shape[1]
    nt = s_len // tm
    out_spec = pl.BlockSpec((1, A_W, tm), lambda i: (i // nt, 0, i % nt))
    return pl.pallas_call(
        _proj_kt_kernel,
        grid=(bsz * nt,),
        in_specs=[pl.BlockSpec((tm, k), lambda i: (i, 0)),
                  pl.BlockSpec((A_W, k), lambda i: (row0, 0)),
                  pl.BlockSpec(nw_col.shape, lambda i: (0, 0, 0))],
        out_specs=[out_spec, out_spec],
        out_shape=[jax.ShapeDtypeStruct((bsz, A_W, s_len), F32),
                   jax.ShapeDtypeStruct((bsz, A_W, s_len), BF16)],
        scratch_shapes=[pltpu.VMEM((A_W, k), BF16)],
        compiler_params=_cparams(("arbitrary",)),
        name="proj_kt",
    )(h, wt, nw_col)


def _proj(kind, h, wseg, extras, out_dtypes, tm, tn):
    w, row0, n = wseg[:3]
    shift = wseg[3] if len(wseg) > 3 else 0
    m, k = h.shape
    extra_specs = [pl.BlockSpec(e.shape, lambda j, i, nd=e.ndim: (0,) * nd) for e in extras]
    if kind in ("q", "k"):
        extra_specs[1] = pl.BlockSpec((1, tn), lambda j, i: (0, j))
    w_specs = [pl.BlockSpec((tn, k), lambda j, i: (row0 + j, 0))]
    if shift:
        w_specs.append(pl.BlockSpec((LANES, k), lambda j, i: ((row0 + j + 1) * (tn // LANES), 0)))
    outs = pl.pallas_call(
        functools.partial(_proj_kernel, kind, shift),
        grid=(n // tn, m // tm),
        in_specs=[pl.BlockSpec((tm, k), lambda j, i: (i, 0))] + w_specs + extra_specs,
        out_specs=[pl.BlockSpec((tm, tn), lambda j, i: (i, j)) for _ in out_dtypes],
        out_shape=[jax.ShapeDtypeStruct((m, n), dt) for dt in out_dtypes],
        scratch_shapes=[pltpu.VMEM((tn, k), BF16)],
        compiler_params=_cparams(("parallel", "arbitrary")),
        name="proj_" + kind,
    )(h, *([w] * len(w_specs)), *extras)
    return outs


def _lam(lamp_ref, lambda_init):
    lp = lamp_ref[...]
    return (jnp.exp(jnp.sum(lp[0:1] * lp[1:2], axis=-1, keepdims=True))
            - jnp.exp(jnp.sum(lp[2:3] * lp[3:4], axis=-1, keepdims=True)) + lambda_init)


def _subln(o, sw, lambda_init):
    ms = jnp.mean(o * o, axis=-1, keepdims=True)
    return o * lax.rsqrt(ms + EPS) * sw * (1.0 - lambda_init)


def _attn_kernel(lamp_ref, q_ref, k_ref, v_ref, sw_ref, o_ref, m_sc, l_sc, acc_sc, *, tq, lambda_init):
    i = pl.program_id(2)
    q = q_ref[...]
    lane = lax.broadcasted_iota(jnp.int32, q.shape, 1)
    zero = jnp.zeros_like(q)
    qs = (jnp.where(lane < HEAD_DIM_A, q, zero), jnp.where(lane >= HEAD_DIM_A, q, zero))
    m_sc[...] = jnp.full(m_sc.shape, -jnp.inf, F32)
    l_sc[...] = jnp.zeros(l_sc.shape, F32)
    acc_sc[...] = jnp.zeros(acc_sc.shape, F32)

    def step(start, width, masked):
        kb = k_ref[0, :, pl.ds(start, width)]
        vb = v_ref[pl.ds(start, width), :]
        for c in range(2):
            s = _dot(qs[c], kb)
            if masked:
                row = lax.broadcasted_iota(jnp.int32, s.shape, 0)
                col = lax.broadcasted_iota(jnp.int32, s.shape, 1)
                s = jnp.where(col <= row, s, -jnp.inf)
            m_prev = m_sc[c]
            m_new = jnp.maximum(m_prev, jnp.max(s, axis=-1, keepdims=True))
            alpha = jnp.exp(m_prev - m_new)
            ps = [jnp.exp(s[:, g * LANES:(g + 1) * LANES] - m_new) for g in range(width // LANES)]
            l_sc[c] = alpha * l_sc[c] + functools.reduce(lambda a, b: a + b, ps)
            p = jnp.concatenate([x.astype(BF16) for x in ps], axis=1)
            acc_sc[c] = alpha * acc_sc[c] + _dot(p, vb)
            m_sc[c] = m_new

    def quad_body(jq, carry):
        step(pl.multiple_of(jq * 4 * tq, 4 * tq), 4 * tq, False)
        return carry

    lax.fori_loop(0, i // 4, quad_body, 0)

    @pl.when(i % 4 >= 2)
    def _():
        step(pl.multiple_of((i // 4) * 4 * tq, 2 * tq), 2 * tq, False)

    @pl.when(i % 2 == 1)
    def _():
        step(pl.multiple_of((i - 1) * tq, tq), tq, False)

    step(pl.multiple_of(i * tq, tq), tq, True)
    lam = _lam(lamp_ref, lambda_init)
    l0 = jnp.sum(l_sc[0], axis=-1, keepdims=True)
    l1 = jnp.sum(l_sc[1], axis=-1, keepdims=True)
    o = acc_sc[0] / l0 - lam * (acc_sc[1] / l1)
    o_ref[...] = _subln(o, sw_ref[...], lambda_init)


def _attn_prompt(q, k, v, lamp, subln_w, bsz, s_len, tq, lambda_init):
    m = q.shape[0]
    nq = s_len // tq
    return pl.pallas_call(
        functools.partial(_attn_kernel, tq=tq, lambda_init=lambda_init),
        grid=(bsz, N_HEADS_A, nq),
        in_specs=[pl.BlockSpec(lamp.shape, lambda b, h, i: (0, 0)),
                  pl.BlockSpec((tq, LANES), lambda b, h, i: (b * nq + i, h)),
                  pl.BlockSpec((1, LANES, s_len), lambda b, h, i: (b, h, 0)),
                  pl.BlockSpec((s_len, LANES), lambda b, h, i: (b, h)),
                  pl.BlockSpec((1, LANES), lambda b, h, i: (0, 0))],
        out_specs=pl.BlockSpec((tq, LANES), lambda b, h, i: (b * nq + i, h)),
        out_shape=jax.ShapeDtypeStruct((m, A_W), F32),
        scratch_shapes=[pltpu.VMEM((2, tq, LANES), F32)] * 3,
        compiler_params=_cparams(("parallel", "parallel", "parallel")),
        name="attn_prompt",
    )(lamp, q, k, v, subln_w.reshape(1, LANES))


def _decode_stages(qbd, e_ref, k_refs, v_refs, state, res):
    m_prev, l_prev, acc_prev = state
    parts = []
    pair = 2 if len(k_refs) % 2 == 0 else 1
    for p_i in range(0, len(k_refs), pair):
        kt = jnp.concatenate([k_refs[p_i + x][0].astype(BF16) for x in range(pair)], axis=1)
        parts.append(_dot(qbd, kt))
        yield
    s = jnp.concatenate(parts, axis=1)
    m_new = jnp.maximum(m_prev, jnp.max(s, axis=-1, keepdims=True))
    alpha = jnp.exp(m_prev - m_new)
    p = jnp.exp(s - m_new)
    res["m"] = m_new
    res["l"] = alpha * l_prev + jnp.sum(p, axis=-1, keepdims=True)
    nrow = p.shape[0]
    p_rows = jnp.concatenate([p[:, i * PAGE_SIZE:(i + 1) * PAGE_SIZE] for i in range(len(v_refs))], axis=0)
    pe_all = _dot(p_rows.astype(BF16), e_ref[...])
    yield
    pv = jnp.zeros(acc_prev.shape, F32)
    r_i = lax.broadcasted_iota(jnp.int32, (nrow, PAGE_SIZE * N_HEADS_A), 0)
    c_i = lax.broadcasted_iota(jnp.int32, (nrow, PAGE_SIZE * N_HEADS_A), 1)
    own = c_i % N_HEADS_A == r_i // 2
    for p_i, v_ref in enumerate(v_refs):
        pe = jnp.where(own, pe_all[p_i * nrow:(p_i + 1) * nrow], 0.0).astype(BF16)
        v2 = v_ref[0].reshape(PAGE_SIZE * N_HEADS_A, LANES).astype(BF16)
        pv = pv + _dot(pe, v2)
        yield
    res["acc"] = alpha * acc_prev + pv


def _dec_attn_kernel(pt_ref, lamp_ref, q_ref, kn_ref, vn_ref, sw_ref, e_ref, *refs, n_pages_step, lambda_init,
                     delta):
    k_refs = refs[:n_pages_step]
    v_refs = refs[n_pages_step:2 * n_pages_step]
    rest = refs[2 * n_pages_step:]
    if delta is None:
        o_ref, m_sc, l_sc, acc_sc = rest
    else:
        d_in = rest[:N_DELTA_IN]
        gb_ref, zb_ref, onw_ref = d_in[9:]
        o_ref, ob_ref, s_ref, m_sc, l_sc, acc_sc, ext_ref = rest[N_DELTA_IN:]
    j = pl.program_id(1)
    nrow = 2 * N_HEADS_A
    q = q_ref[0]
    hc = lax.broadcasted_iota(jnp.int32, (nrow, A_W), 0)
    lane = lax.broadcasted_iota(jnp.int32, (nrow, A_W), 1)
    qbd = jnp.where(lane // HEAD_DIM_A == hc, jnp.broadcast_to(q, (nrow, A_W)), 0.0).astype(BF16)

    @pl.when(j == 0)
    def _():
        kn = jnp.broadcast_to(kn_ref[0], (nrow, A_W)).astype(BF16).astype(F32)
        m_sc[...] = jnp.sum(qbd.astype(F32) * kn, axis=-1, keepdims=True)
        l_sc[...] = jnp.ones(l_sc.shape, F32)
        vn = vn_ref[0].astype(BF16).astype(F32)
        row_head = lax.broadcasted_iota(jnp.int32, (nrow, LANES), 0) // 2
        acc = jnp.zeros((nrow, LANES), F32)
        for h in range(N_HEADS_A):
            acc = jnp.where(row_head == h, jnp.broadcast_to(vn[:, h * LANES:(h + 1) * LANES], (nrow, LANES)), acc)
        acc_sc[...] = acc

    res = {}
    gens = [_decode_stages(qbd, e_ref, k_refs, v_refs, (m_sc[...], l_sc[...], acc_sc[...]), res)]
    if delta is not None:
        hpu, ng, nc, n_units = delta
        t = pl.program_id(0) * pl.num_programs(1) + j
        unit = jnp.minimum(t, n_units - 1)
        h0 = pl.multiple_of((unit % ng) * hpu, hpu)

        first_chunk = (unit // ng) % nc == 0

        @pl.when(first_chunk & (t < n_units))
        def _():
            s_ref[0, pl.ds(h0, hpu)] = jnp.zeros((hpu, DK_B, DV_B), F32)

        dres = {}

        def conv_then_delta():
            dq, dk, dv = _conv_unit(d_in[0:3], d_in[3:6], d_in[6:9], ext_ref, first_chunk, hpu)
            yield
            yield from _delta_stages(dq, dk, dv, gb_ref[0], s_ref[0, pl.ds(h0, hpu)], hpu, dres)

        gens.append(conv_then_delta())
    _interleave(*gens)
    m_sc[...] = res["m"]
    l_sc[...] = res["l"]
    acc_sc[...] = res["acc"]
    if delta is not None:
        slabs = _delta_finish(dres["o"], zb_ref, onw_ref, hpu)

        @pl.when(t < n_units)
        def _():
            s_ref[0, pl.ds(h0, hpu)] = dres["s"]
            for h, slab in enumerate(slabs):
                ob_ref[:, h * LANES:(h + 1) * LANES] = slab

    @pl.when(j == pl.num_programs(1) - 1)
    def _():
        lam = _lam(lamp_ref, lambda_init)
        o = acc_sc[...] / l_sc[...]
        for h in range(N_HEADS_A):
            oh = o[2 * h:2 * h + 1] - lam * o[2 * h + 1:2 * h + 2]
            o_ref[0, :, h * LANES:(h + 1) * LANES] = _subln(oh, sw_ref[...], lambda_init)


def _delta_units(bsz, s_len, n_steps):
    nc = s_len // CHUNK
    for hpu in (1, 2, 4, 8):
        if bsz * nc * (N_HEADS_B // hpu) <= n_steps:
            return hpu
    return None


def _attn_sample(q, kn, vn, cache_kt, cache_v, page_table, lamp, subln_w, lambda_init, delta_args=None):
    nb, n_pages = page_table.shape
    pps = PAGES_PER_STEP if n_pages % PAGES_PER_STEP == 0 else 1
    nj = n_pages // pps
    row_spec = pl.BlockSpec((1, 1, A_W), lambda b, j, pt: (b, 0, 0))
    d_in_specs, d_out_specs, d_out_shapes, d_args, d_scratch, delta = [], [], [], [], [], None
    if delta_args is not None:
        raw, conv_w, gb, zb, o_norm_w, bsz, s_len, hpu = delta_args
        nc = s_len // CHUNK
        ng = N_HEADS_B // hpu
        n_units = bsz * nc * ng
        delta = (hpu, ng, nc, n_units)
        unit = lambda b, j: jnp.minimum(b * nj + j, n_units - 1)
        d_in_specs, d_args, ob_spec, ext = _delta_operands(
            raw, conv_w, gb, zb, o_norm_w, hpu, lambda b, j, pt: (unit(b, j) // ng, unit(b, j) % ng))
        d_out_specs = [ob_spec, pl.BlockSpec((1, N_HEADS_B, DK_B, DV_B),
                                             lambda b, j, pt: (unit(b, j) // (ng * nc), 0, 0, 0))]
        d_out_shapes = [jax.ShapeDtypeStruct((raw.shape[0], B_W), BF16),
                        jax.ShapeDtypeStruct((bsz, N_HEADS_B, DK_B, DV_B), F32)]
        d_scratch = [ext]

    def page_spec(p_i, shape):
        return pl.BlockSpec((1,) + shape, lambda b, j, pt: (pt[b, j * pps + p_i],) + (0,) * len(shape))

    expand = (jnp.arange(PAGE_SIZE)[:, None] == jnp.arange(PAGE_SIZE * N_HEADS_A)[None, :] // N_HEADS_A).astype(BF16)
    grid_spec = pltpu.PrefetchScalarGridSpec(
        num_scalar_prefetch=1,
        grid=(nb, nj),
        in_specs=[pl.BlockSpec(lamp.shape, lambda b, j, pt: (0, 0)), row_spec, row_spec, row_spec,
                  pl.BlockSpec((1, LANES), lambda b, j, pt: (0, 0)),
                  pl.BlockSpec(expand.shape, lambda b, j, pt: (0, 0))]
                 + [page_spec(p_i, (A_W, PAGE_SIZE)) for p_i in range(pps)]
                 + [page_spec(p_i, (PAGE_SIZE, N_HEADS_A, LANES)) for p_i in range(pps)]
                 + d_in_specs,
        out_specs=[row_spec] + d_out_specs,
        scratch_shapes=[pltpu.VMEM((2 * N_HEADS_A, 1), F32), pltpu.VMEM((2 * N_HEADS_A, 1), F32),
                        pltpu.VMEM((2 * N_HEADS_A, LANES), F32)] + d_scratch,
    )
    outs = pl.pallas_call(
        functools.partial(_dec_attn_kernel, n_pages_step=pps, lambda_init=lambda_init, delta=delta),
        grid_spec=grid_spec,
        out_shape=[jax.ShapeDtypeStruct((nb, 1, A_W), F32)] + d_out_shapes,
        compiler_params=_cparams(("arbitrary", "arbitrary")),
        name="attn_sample",
    )(page_table, lamp, q, kn, vn, subln_w.reshape(1, LANES), expand,
      *([cache_kt] * pps), *([cache_v] * pps), *d_args)
    return outs if delta_args is not None else outs[0]


def _l2n(y):
    return y * lax.rsqrt(jnp.sum(y * y, axis=-1, keepdims=True) + EPS)


def _conv_unit(raw_refs, prev_refs, w_refs, ext_ref, first_chunk, nh):
    hist = SUBLANES - (CONV_W - 1)
    outs = []
    for s in range(3):
        ext_ref[s, 0:SUBLANES, :] = jnp.where(first_chunk, 0.0, prev_refs[s][...])
        ext_ref[s, SUBLANES:SUBLANES + CHUNK, :] = raw_refs[s][...]
        w = w_refs[s][...]
        y = ext_ref[s, hist:hist + CHUNK, :] * w[0:1]
        for jj in range(1, CONV_W):
            y = y + ext_ref[s, hist + jj:hist + jj + CHUNK, :] * w[jj:jj + 1]
        y = _silu(y)
        slabs = [y[:, h * LANES:(h + 1) * LANES] for h in range(nh)]
        if s == 0:
            slabs = [_l2n(x) * (DK_B ** -0.5) for x in slabs]
        elif s == 1:
            slabs = [_l2n(x) for x in slabs]
        outs.append(jnp.stack(slabs))
    return outs


def _bdot(a, b):
    return lax.dot_general(a, b, (((2,), (1,)), ((0,), (0,))), preferred_element_type=F32)


def _bdot_nt(a, b):
    return lax.dot_general(a, b, (((2,), (2,)), ((0,), (0,))), preferred_element_type=F32)


def _bdot_tn(a, b):
    return lax.dot_general(a, b, (((1,), (1,)), ((0,), (0,))), preferred_element_type=F32)


def _split_bdot(a, b):
    ah = a.astype(BF16)
    al = (a - ah.astype(F32)).astype(BF16)
    bh = b.astype(BF16)
    bl = (b - bh.astype(F32)).astype(BF16)
    return _bdot(ah, bh) + _bdot(ah, bl) + _bdot(al, bh)


def _cumsum_rows(x):
    n = x.shape[0]
    row = lax.broadcasted_iota(jnp.int32, x.shape, 0)
    s = 1
    while s < n:
        x = x + jnp.where(row >= s, pltpu.roll(x, s, axis=0), 0.0)
        s *= 2
    return x


def _delta_stages(q, k, v, gb, s_old, nh, res):
    lane_bcast = lambda x, off: jnp.stack(
        [jnp.broadcast_to(x[:, off + h:off + h + 1], (CHUNK, LANES)) for h in range(nh)])
    gc_all = _cumsum_rows(gb)
    gc_t = gc_all.T
    gc = lane_bcast(gc_all, 0)
    gc_row = jnp.stack([gc_t[h:h + 1, :] for h in range(nh)])
    beta = lane_bcast(gb, nh)
    row = lax.broadcasted_iota(jnp.int32, (1, CHUNK, CHUNK), 1)
    col = lax.broadcasted_iota(jnp.int32, (1, CHUNK, CHUNK), 2)
    tril = col <= row
    strict = col < row
    eye = jnp.where(row == col, 1.0, 0.0)
    decay = jnp.exp(jnp.where(tril, gc[:, :, :CHUNK] - gc_row, -jnp.inf))
    kb = k * beta
    k16 = k.astype(BF16)
    lmat = jnp.where(strict, _bdot_nt(kb.astype(BF16), k16) * decay, 0.0)
    qk = jnp.where(tril, _bdot_nt(q.astype(BF16), k16) * decay, 0.0)
    yield
    tinv = eye - lmat
    pw = lmat
    for _ in range(5):
        pw = _split_bdot(pw, pw)
        yield
        tinv = tinv + _split_bdot(tinv, pw)
        yield
    t16 = tinv.astype(BF16)
    u = _bdot(t16, (v * beta).astype(BF16))
    wk = _bdot(t16, (kb * jnp.exp(gc)).astype(BF16))
    s16 = s_old.astype(BF16)
    yield
    v_new = u - _bdot(wk.astype(BF16), s16)
    vn16 = v_new.astype(BF16)
    yield
    res["o"] = _bdot((q * jnp.exp(gc)).astype(BF16), s16) + _bdot(qk.astype(BF16), vn16)
    g_last = gc[:, CHUNK - 1:CHUNK, :]
    k_dec = k * jnp.exp(g_last - gc)
    res["s"] = s_old * jnp.exp(g_last) + _bdot_tn(k_dec.astype(BF16), vn16)


def _interleave(*gens):
    live = list(gens)
    while live:
        for g in list(live):
            try:
                next(g)
            except StopIteration:
                live.remove(g)


def _delta_finish(o, zb_ref, onw_ref, nh):
    on = o * lax.rsqrt(jnp.mean(o * o, axis=-1, keepdims=True) + EPS) * onw_ref[...]
    return [(on[h] * zb_ref[:, h * LANES:(h + 1) * LANES]).astype(BF16) for h in range(nh)]


N_DELTA_IN = 12


def _delta_kernel(*refs):
    d_in = refs[:N_DELTA_IN]
    ob_ref, s_ref, ext_ref = refs[N_DELTA_IN:]
    gb_ref, zb_ref, onw_ref = d_in[9:]
    c = pl.program_id(1)
    nh = N_HEADS_B

    @pl.when(c == 0)
    def _():
        s_ref[...] = jnp.zeros(s_ref.shape, F32)

    q, k, v = _conv_unit(d_in[0:3], d_in[3:6], d_in[6:9], ext_ref, c == 0, nh)
    res = {}
    _interleave(_delta_stages(q, k, v, gb_ref[0], s_ref[0], nh, res))
    s_ref[0] = res["s"]
    for h, slab in enumerate(_delta_finish(res["o"], zb_ref, onw_ref, nh)):
        ob_ref[:, h * LANES:(h + 1) * LANES] = slab


def _delta_operands(raw, conv_w, gb, zb, o_norm_w, hpu, where):
    ng = N_HEADS_B // hpu
    wd = hpu * LANES
    rc = lambda *a: where(*a)[0]
    hg = lambda *a: where(*a)[1]
    cur = lambda s: pl.BlockSpec((CHUNK, wd), lambda *a, s=s: (rc(*a), s * ng + hg(*a)))
    prev = lambda s: pl.BlockSpec(
        (SUBLANES, wd), lambda *a, s=s: (jnp.maximum(rc(*a) * (CHUNK // SUBLANES) - 1, 0), s * ng + hg(*a)))
    taps = lambda s: pl.BlockSpec((CONV_W, wd), lambda *a, s=s: (0, s * ng + hg(*a)))
    specs = ([cur(s) for s in range(3)] + [prev(s) for s in range(3)] + [taps(s) for s in range(3)]
             + [pl.BlockSpec((1, CHUNK, LANES), lambda *a: (hg(*a), rc(*a), 0)),
                pl.BlockSpec((CHUNK, wd), lambda *a: (rc(*a), hg(*a))),
                pl.BlockSpec((1, LANES), lambda *a: (0, 0))])
    args = [raw] * 6 + [conv_w] * 3 + [_group_gates(gb, hpu), zb, o_norm_w.reshape(1, LANES)]
    out_spec = pl.BlockSpec((CHUNK, wd), lambda *a: (rc(*a), hg(*a)))
    scratch = pltpu.VMEM((3, CHUNK + SUBLANES, wd), F32)
    return specs, args, out_spec, scratch


def _group_gates(gb, hpu):
    m = gb.shape[0]
    ng = N_HEADS_B // hpu
    g = gb[:, :N_HEADS_B].reshape(m, ng, hpu)
    beta = gb[:, N_HEADS_B:2 * N_HEADS_B].reshape(m, ng, hpu)
    gbg = jnp.transpose(jnp.concatenate([g, beta], axis=-1), (1, 0, 2))
    return jnp.pad(gbg, ((0, 0), (0, 0), (0, LANES - 2 * hpu)))


def _delta_prompt(raw, conv_w, gb, zb, o_norm_w, bsz, s_len):
    m = raw.shape[0]
    nc = s_len // CHUNK
    specs, args, out_spec, scratch = _delta_operands(raw, conv_w, gb, zb, o_norm_w, N_HEADS_B,
                                                     lambda b, c: (b * nc + c, 0))
    return pl.pallas_call(
        _delta_kernel,
        grid=(bsz, nc),
        in_specs=specs,
        out_specs=[out_spec, pl.BlockSpec((1, N_HEADS_B, DK_B, DV_B), lambda b, c: (b, 0, 0, 0))],
        out_shape=[jax.ShapeDtypeStruct((m, B_W), BF16),
                   jax.ShapeDtypeStruct((bsz, N_HEADS_B, DK_B, DV_B), F32)],
        scratch_shapes=[scratch],
        compiler_params=_cparams(("parallel", "arbitrary")),
        name="delta_prompt",
    )(*args)


def _dec_delta_kernel(u_ref, buf_ref, w_ref, gb_ref, zb_ref, onw_ref, s0_ref, ob_ref, s_ref):
    w = w_ref[...]
    buf = buf_ref[0]
    y = buf[0:1] * w[0:1] + buf[1:2] * w[1:2] + buf[2:3] * w[2:3] + u_ref[0] * w[3:4]
    y = _silu(y)
    gb = gb_ref[0]
    r = lax.broadcasted_iota(jnp.int32, (DK_B, LANES), 0)
    cidx = lax.broadcasted_iota(jnp.int32, (DK_B, LANES), 1)
    eye = r == cidx

    def column(x):
        return jnp.sum(jnp.where(eye, jnp.broadcast_to(x, (DK_B, LANES)), 0.0), axis=-1, keepdims=True)

    for h in range(N_HEADS_B):
        sl = slice(h * LANES, (h + 1) * LANES)
        q = _l2n(y[:, h * LANES:(h + 1) * LANES]) * (DK_B ** -0.5)
        k = _l2n(y[:, B_W + h * LANES:B_W + (h + 1) * LANES])
        v = y[:, 2 * B_W + h * LANES:2 * B_W + (h + 1) * LANES]
        a = jnp.exp(gb[:, h:h + 1])
        beta = gb[:, N_HEADS_B + h:N_HEADS_B + h + 1]
        s_old = s0_ref[0, h]
        k_col = column(k)
        q_col = column(q)
        v_new = beta * (v - a * jnp.sum(k_col * s_old, axis=0, keepdims=True))
        s_new = a * s_old + k_col * v_new
        s_ref[0, h] = s_new
        o = jnp.sum(q_col * s_new, axis=0, keepdims=True)
        on = o * lax.rsqrt(jnp.mean(o * o, axis=-1, keepdims=True) + EPS) * onw_ref[...]
        ob_ref[0, :, sl] = (on * zb_ref[0, :, sl]).astype(BF16)


def _delta_sample(u, conv_buf, conv_w, gb, zb, o_norm_w, s0):
    nb, _, cdim = u.shape
    row = lambda wd: pl.BlockSpec((1, 1, wd), lambda b: (b, 0, 0))
    st = pl.BlockSpec((1, N_HEADS_B, DK_B, DV_B), lambda b: (b, 0, 0, 0))
    return pl.pallas_call(
        _dec_delta_kernel,
        grid=(nb,),
        in_specs=[row(cdim), pl.BlockSpec((1, CONV_W - 1, cdim), lambda b: (b, 0, 0)),
                  pl.BlockSpec((CONV_W, cdim), lambda b: (0, 0)), row(LANES), row(B_W),
                  pl.BlockSpec((1, LANES), lambda b: (0, 0)), st],
        out_specs=[row(B_W), st],
        out_shape=[jax.ShapeDtypeStruct((nb, 1, B_W), BF16),
                   jax.ShapeDtypeStruct(s0.shape, F32)],
        compiler_params=_cparams(("parallel",)),
        name="delta_sample",
    )(u, conv_buf, conv_w, gb, zb, o_norm_w.reshape(1, LANES), s0)


def _out_kernel(oa_ref, za_ref, ob_ref, ga_ref, gbt_ref, x_ref, gate_ref, wa_ref, wb_ref, wo_ref, y_ref):
    ya = _dot((oa_ref[...] * za_ref[...]).astype(BF16), wa_ref[...])
    yb = _dot(ob_ref[...], wb_ref[...])
    mrg = ga_ref[...] * ya + gbt_ref[...] * yb
    y_ref[0] = x_ref[0] + gate_ref[0] * _dot(mrg.astype(BF16), wo_ref[...])


def _out_proj(oa, za, ob, gates, x, gate, wa, wb, wo, tm):
    b, s, d = x.shape
    nt = s // tm
    ts = tm if gate.shape[1] == s else 1
    gate_map = (lambda i, j: (i, j, 0)) if ts == tm else (lambda i, j: (i, 0, 0))
    rows = lambda wd, cb=0: pl.BlockSpec((tm, wd), lambda i, j, cb=cb: (i * nt + j, cb))
    const = lambda shp: pl.BlockSpec(shp, lambda i, j: (0, 0), pipeline_mode=pl.Buffered(1))
    return pl.pallas_call(
        _out_kernel,
        grid=(b, nt),
        in_specs=[rows(A_W), rows(A_W), rows(B_W), rows(d, 0), rows(d, 1),
                  pl.BlockSpec((1, tm, d), lambda i, j: (i, j, 0)),
                  pl.BlockSpec((1, ts, d), gate_map),
                  const(wa.shape), const(wb.shape), const(wo.shape)],
        out_specs=pl.BlockSpec((1, tm, d), lambda i, j: (i, j, 0)),
        out_shape=jax.ShapeDtypeStruct((b, s, d), F32),
        compiler_params=_cparams(("parallel", "parallel")),
        name="out_proj",
    )(oa, za, ob, gates, gates, x, gate, wa, wb, wo)


def _tile(n, pref):
    t = min(n, pref)
    while n % t:
        t //= 2
    return t


def _project(h2, wts, tm, kt_dims=None):
    (w_main, cdim, gmat, qnw, knw, alog, dtb) = wts
    tn = A_W
    seg = lambda col, n: (w_main, col // tn, n)
    (q16,) = _proj("q", h2, seg(0, A_W), [gmat, qnw], [BF16], tm, tn)
    if kt_dims is None:
        k32, k16 = _proj("k", h2, seg(A_W, A_W), [gmat, knw], [F32, BF16], tm, tn)
    else:
        knw_col = knw[0, :HEAD_DIM_A].reshape(1, HEAD_DIM_A, 1)
        k32, k16 = _proj_kt(h2, (w_main, 1), knw_col, kt_dims[0], kt_dims[1], tm)
    v32, v16 = _proj("v", h2, seg(2 * A_W, A_W), [], [F32, BF16], tm, tn)
    (za,) = _proj("silu", h2, seg(3 * A_W, A_W), [], [F32], tm, tn)
    (raw,) = _proj("raw", h2, seg(4 * A_W, cdim), [], [F32], tm, tn)
    (zb,) = _proj("silu", h2, seg(4 * A_W + cdim, B_W), [], [F32], tm, tn)
    o_ab = 4 * A_W + cdim + B_W
    (gb,) = _proj("ab", h2, (w_main, o_ab // LANES, LANES), [alog, dtb], [F32], tm, LANES)
    n_gates = w_main.shape[0] - o_ab - 2 * N_HEADS_B
    (gates,) = _proj("sigmoid", h2, (w_main, o_ab // tn, n_gates, 2 * N_HEADS_B), [], [F32], tm, tn)
    return q16, k32, k16, v32, v16, za, raw, zb, gb, gates


def kernel(x_prompt, x_sample, c_prompt, c_sample, cache_k, cache_v, page_table, state_ssm, state_conv, w_ada, b_ada, norm_w, w_in, q_norm_w, k_norm_w, lam_q1, lam_k1, lam_q2, lam_k2, subln_w, conv_w, a_log, dt_bias, o_norm_w, w_branch_a, w_branch_b, w_out):
    depth = w_ada.shape[0]
    bsz, s_len, d = x_prompt.shape
    nb, t_dec, _ = x_sample.shape
    assert t_dec == 1 and s_len % CHUNK == 0 and s_len >= CONV_W - 1
    m_p = bsz * s_len
    cdim = conv_w.shape[-1]
    o_ab = 4 * A_W + cdim + B_W
    assert cdim % A_W == 0 and w_in.shape[-1] == o_ab + 2 * N_HEADS_B + 2 * d and (2 * d) % A_W == 0

    gi = jnp.arange(LANES) // HEAD_DIM_A
    gmat = jnp.where(gi[:, None] == gi[None, :], 1.0 / HEAD_DIM_A, 0.0).astype(BF16)

    yp, ys = x_prompt, x_sample
    outs = [[] for _ in range(8)]
    for li in range(depth):
        lambda_init = 0.8 - 0.6 * math.exp(-0.3 * li)
        w_main = w_in[li].T
        pad8 = lambda vec: jnp.pad(vec, (0, LANES - N_HEADS_B)).reshape(1, LANES)
        wts = (w_main, cdim, gmat,
               jnp.tile(q_norm_w[li], 2 * N_HEADS_A).reshape(1, A_W),
               jnp.tile(k_norm_w[li], 2 * N_HEADS_A).reshape(1, A_W),
               pad8(a_log[li]), pad8(dt_bias[li]))
        wa16 = w_branch_a[li].astype(BF16)
        wbb16 = w_branch_b[li].astype(BF16)
        wo16 = w_out[li].astype(BF16)
        lamp = jnp.stack([lam_q1[li], lam_k1[li], lam_q2[li], lam_k2[li]])

        n_c = bsz + nb
        c_all = jnp.pad(jnp.concatenate([c_prompt, c_sample], axis=0), ((0, (-n_c) % 16), (0, 0)))
        mod = _ada(c_all, w_ada[li], b_ada[li])
        shift, scale, gate = mod[:, :d], mod[:, d:2 * d], mod[:, 2 * d:]

        tm = _tile(s_len, 1024)
        h = _hnorm(yp, shift[:bsz, None], scale[:bsz, None], norm_w[li], tm)
        q16, kt32, kt16, v32, v16, za, raw, zb, gb, gates = _project(h.reshape(m_p, d), wts, tm, (bsz, s_len))
        xs = ys.reshape(1, nb, d)
        sl_s = slice(bsz, bsz + nb)
        h_s = _hnorm(xs, shift[None, sl_s], scale[None, sl_s], norm_w[li], nb)
        q16_s, k32_s, _, v32_s, _, za_s, raw_s, zb_s, gb_s, gates_s = _project(h_s.reshape(nb, d), wts, nb)

        oa = _attn_prompt(q16, kt16, v16, lamp, subln_w[li], bsz, s_len, _tile(s_len, 512), lambda_init)
        n_pool = cache_k.shape[1]
        cache_kt = jnp.transpose(cache_k[li], (0, 2, 3, 4, 1)).reshape(n_pool, A_W, PAGE_SIZE)
        dec_args = (q16_s.astype(F32).reshape(nb, 1, A_W), k32_s.reshape(nb, 1, A_W), v32_s.reshape(nb, 1, A_W),
                    cache_kt, cache_v[li], page_table, lamp, subln_w[li], lambda_init)
        n_pages = page_table.shape[1]
        n_steps = nb * (n_pages // (PAGES_PER_STEP if n_pages % PAGES_PER_STEP == 0 else 1))
        hpu = _delta_units(bsz, s_len, n_steps)
        if hpu is None:
            oa_s = _attn_sample(*dec_args)
            ob, ssm_p = _delta_prompt(raw, conv_w[li], gb, zb, o_norm_w[li], bsz, s_len)
        else:
            oa_s, ob, ssm_p = _attn_sample(
                *dec_args, delta_args=(raw, conv_w[li], gb, zb, o_norm_w[li], bsz, s_len, hpu))
        ob_s, ssm_s = _delta_sample(raw_s.reshape(nb, 1, cdim), state_conv[li], conv_w[li],
                                    gb_s.reshape(nb, 1, LANES), zb_s.reshape(nb, 1, B_W), o_norm_w[li],
                                    state_ssm[li])

        yp = _out_proj(oa, za, ob, gates, yp, gate[:bsz, None], wa16, wbb16, wo16, _tile(s_len, 256))
        ys = _out_proj(oa_s.reshape(nb, A_W), za_s, ob_s.reshape(nb, B_W), gates_s, xs, gate[None, sl_s],
                       wa16, wbb16, wo16, nb).reshape(nb, 1, d)
        outs[0].append(jnp.transpose(kt32.reshape(bsz, N_HEADS_A, 2, HEAD_DIM_A, s_len), (0, 4, 1, 2, 3)))
        outs[1].append(v32.reshape(bsz, s_len, N_HEADS_A, 2 * HEAD_DIM_A))
        outs[2].append(k32_s.reshape(nb, 1, N_HEADS_A, 2, HEAD_DIM_A))
        outs[3].append(v32_s.reshape(nb, 1, N_HEADS_A, 2 * HEAD_DIM_A))
        outs[4].append(ssm_p)
        outs[5].append(ssm_s)
        outs[6].append(raw.reshape(bsz, s_len, cdim)[:, s_len - (CONV_W - 1):])
        outs[7].append(jnp.concatenate([state_conv[li][:, 1:], raw_s.reshape(nb, 1, cdim)], axis=1))

    st = [jnp.stack(o) for o in outs]
    return (yp, ys, st[0], st[1], st[2], st[3], st[4], st[5], st[6], st[7])
```

```python
import functools
import math

import jax
import jax.numpy as jnp
from jax import lax
from jax.experimental import pallas as pl
from jax.experimental.pallas import tpu as pltpu

F32 = jnp.float32
BF16 = jnp.bfloat16

N_HEADS_A = 8
HEAD_DIM_A = 64
A_W = N_HEADS_A * 2 * HEAD_DIM_A
N_HEADS_B = 8
DK_B = 128
DV_B = 128
B_W = N_HEADS_B * DK_B
CONV_W = 4
CHUNK = 64
PAGE_SIZE = 128
EPS = 1e-6
LANES = 128
SUBLANES = 8
VMEM_LIMIT = 56 * 1024 * 1024
PAGES_PER_STEP = 16
PROJ_SUBTILES = 4
SMALL_M_TN = 256

_NT = (((1,), (1,)), ((), ()))
_TN = (((0,), (0,)), ((), ()))


def _cparams(sem):
    return pltpu.CompilerParams(dimension_semantics=sem, vmem_limit_bytes=VMEM_LIMIT)


def _dot(a, b):
    return jnp.dot(a, b, preferred_element_type=F32)


def _dot_nt(a, b):
    return lax.dot_general(a, b, _NT, preferred_element_type=F32)


def _dot_tn(a, b):
    return lax.dot_general(a, b, _TN, preferred_element_type=F32)


def _sigmoid(x):
    return 1.0 / (1.0 + jnp.exp(-x))


def _silu(x):
    return x * _sigmoid(x)


def _ada_kernel(c_ref, w_ref, b_ref, o_ref):
    sc = _silu(c_ref[...]).astype(BF16)
    o_ref[...] = _dot(sc, w_ref[...].astype(BF16)) + b_ref[...]


def _ada(c, w_ada, b_ada):
    m, d = c.shape
    n = w_ada.shape[1]
    tn = 512
    return pl.pallas_call(
        _ada_kernel,
        grid=(n // tn,),
        in_specs=[pl.BlockSpec((m, d), lambda j: (0, 0)),
                  pl.BlockSpec((d, tn), lambda j: (0, j)),
                  pl.BlockSpec((1, tn), lambda j: (0, j))],
        out_specs=pl.BlockSpec((m, tn), lambda j: (0, j)),
        out_shape=jax.ShapeDtypeStruct((m, n), F32),
        compiler_params=_cparams(("parallel",)),
        name="ada",
    )(c, w_ada, b_ada.reshape(1, n))


def _hnorm_kernel(x_ref, shift_ref, scale_ref, nw_ref, h_ref):
    x = x_ref[0]
    ms = jnp.mean(x * x, axis=-1, keepdims=True)
    y = x * lax.rsqrt(ms + EPS) * nw_ref[...]
    h_ref[0] = (y * (1.0 + scale_ref[0]) + shift_ref[0]).astype(BF16)


def _hnorm(x, shift, scale, norm_w, tm):
    b, s, d = x.shape
    ts = tm if shift.shape[1] == s else 1
    mod_map = (lambda i, j: (i, j, 0)) if ts == tm else (lambda i, j: (i, 0, 0))
    return pl.pallas_call(
        _hnorm_kernel,
        grid=(b, s // tm),
        in_specs=[pl.BlockSpec((1, tm, d), lambda i, j: (i, j, 0)),
                  pl.BlockSpec((1, ts, d), mod_map),
                  pl.BlockSpec((1, ts, d), mod_map),
                  pl.BlockSpec((1, d), lambda i, j: (0, 0))],
        out_specs=pl.BlockSpec((1, tm, d), lambda i, j: (i, j, 0)),
        out_shape=jax.ShapeDtypeStruct((b, s, d), BF16),
        compiler_params=_cparams(("parallel", "parallel")),
        name="hnorm",
    )(x, shift, scale, norm_w.reshape(1, d))


def _group_rms(acc, g_ref):
    sq = acc * acc
    hi = sq.astype(BF16)
    lo = (sq - hi.astype(F32)).astype(BF16)
    g = g_ref[...]
    outs = []
    for hh in range(acc.shape[1] // LANES):
        sl = slice(hh * LANES, (hh + 1) * LANES)
        ms = _dot(hi[:, sl], g) + _dot(lo[:, sl], g)
        outs.append(acc[:, sl] * lax.rsqrt(ms + EPS))
    return outs


def _proj_kernel(kind, shift, h_ref, w_ref, *refs):
    *refs, w16_ref = refs
    tn = w16_ref.shape[0]
    if shift:
        w2_ref, *refs = refs

    @pl.when(pl.program_id(1) == 0)
    def _():
        if shift:
            w16_ref[0:tn - shift, :] = w_ref[shift:tn, :].astype(BF16)
            w16_ref[tn - shift:tn, :] = w2_ref[0:shift, :].astype(BF16)
        else:
            w16_ref[...] = w_ref[...].astype(BF16)

    tm = h_ref.shape[0]
    nsub = PROJ_SUBTILES if tm % (PROJ_SUBTILES * LANES) == 0 else 1
    for r in range(nsub):
        rows = slice(r * (tm // nsub), (r + 1) * (tm // nsub))
        acc = _dot_nt(h_ref[rows, :], w16_ref[...])
        _proj_epilogue(kind, acc, rows, refs)


def _proj_epilogue(kind, acc, rows, refs):
    if kind == "q":
        g_ref, nw_ref, o_ref = refs
        for hh, y in enumerate(_group_rms(acc, g_ref)):
            sl = slice(hh * LANES, (hh + 1) * LANES)
            o_ref[rows, sl] = (y * nw_ref[:, sl] * (HEAD_DIM_A ** -0.5)).astype(BF16)
    elif kind == "k":
        g_ref, nw_ref, o_ref, ob_ref = refs
        for hh, y in enumerate(_group_rms(acc, g_ref)):
            sl = slice(hh * LANES, (hh + 1) * LANES)
            y = y * nw_ref[:, sl]
            o_ref[rows, sl] = y
            ob_ref[rows, sl] = y.astype(BF16)
    elif kind == "v":
        o_ref, ob_ref = refs
        o_ref[rows, :] = acc
        ob_ref[rows, :] = acc.astype(BF16)
    elif kind == "silu":
        (o_ref,) = refs
        o_ref[rows, :] = _silu(acc)
    elif kind == "sigmoid":
        (o_ref,) = refs
        o_ref[rows, :] = _sigmoid(acc)
    elif kind == "raw":
        (o_ref,) = refs
        o_ref[rows, :] = acc
    elif kind == "ab":
        alog_ref, dtb_ref, o_ref = refs
        a = acc + dtb_ref[...]
        sp = jnp.maximum(a, 0.0) + jnp.log(1.0 + jnp.exp(-jnp.abs(a)))
        g = -jnp.exp(alog_ref[...]) * sp
        lane = lax.broadcasted_iota(jnp.int32, acc.shape, 1)
        o_ref[rows, :] = jnp.where(lane < N_HEADS_B, g, _sigmoid(acc))
    else:
        raise ValueError(kind)


def _proj_kt_kernel(h_ref, wt_ref, nw_ref, o_ref, ob_ref, w16_ref):
    @pl.when(pl.program_id(0) == 0)
    def _():
        w16_ref[...] = wt_ref[...].astype(BF16)

    acc = _dot_nt(w16_ref[...], h_ref[...])
    tm = acc.shape[1]
    x3 = acc.reshape(2 * N_HEADS_A, HEAD_DIM_A, tm)
    ms = jnp.mean(x3 * x3, axis=1, keepdims=True)
    y = (x3 * lax.rsqrt(ms + EPS) * nw_ref[...]).reshape(A_W, tm)
    o_ref[0] = y
    ob_ref[0] = y.astype(BF16)


def _proj_kt(h, wseg, nw_col, bsz, s_len, tm):
    wt, row0 = wseg
    k = h.shape[1]
    nt = s_len // tm
    out_spec = pl.BlockSpec((1, A_W, tm), lambda i: (i // nt, 0, i % nt))
    return pl.pallas_call(
        _proj_kt_kernel,
        grid=(bsz * nt,),
        in_specs=[pl.BlockSpec((tm, k), lambda i: (i, 0)),
                  pl.BlockSpec((A_W, k), lambda i: (row0, 0)),
                  pl.BlockSpec(nw_col.shape, lambda i: (0, 0, 0))],
        out_specs=[out_spec, out_spec],
        out_shape=[jax.ShapeDtypeStruct((bsz, A_W, s_len), F32),
                   jax.ShapeDtypeStruct((bsz, A_W, s_len), BF16)],
        scratch_shapes=[pltpu.VMEM((A_W, k), BF16)],
        compiler_params=_cparams(("arbitrary",)),
        name="proj_kt",
    )(h, wt, nw_col)


def _proj(kind, h, wseg, extras, out_dtypes, tm, tn):
    w, row0, n = wseg[:3]
    shift = wseg[3] if len(wseg) > 3 else 0
    m, k = h.shape
    extra_specs = [pl.BlockSpec(e.shape, lambda j, i, nd=e.ndim: (0,) * nd) for e in extras]
    if kind in ("q", "k"):
        extra_specs[1] = pl.BlockSpec((1, tn), lambda j, i: (0, j))
    w_specs = [pl.BlockSpec((tn, k), lambda j, i: (row0 + j, 0))]
    if shift:
        w_specs.append(pl.BlockSpec((LANES, k), lambda j, i: ((row0 + j + 1) * (tn // LANES), 0)))
    outs = pl.pallas_call(
        functools.partial(_proj_kernel, kind, shift),
        grid=(n // tn, m // tm),
        in_specs=[pl.BlockSpec((tm, k), lambda j, i: (i, 0))] + w_specs + extra_specs,
        out_specs=[pl.BlockSpec((tm, tn), lambda j, i: (i, j)) for _ in out_dtypes],
        out_shape=[jax.ShapeDtypeStruct((m, n), dt) for dt in out_dtypes],
        scratch_shapes=[pltpu.VMEM((tn, k), BF16)],
        compiler_params=_cparams(("parallel", "arbitrary")),
        name="proj_" + kind,
    )(h, *([w] * len(w_specs)), *extras)
    return outs


def _lam(lamp_ref, lambda_init):
    lp = lamp_ref[...]
    return (jnp.exp(jnp.sum(lp[0:1] * lp[1:2], axis=-1, keepdims=True))
            - jnp.exp(jnp.sum(lp[2:3] * lp[3:4], axis=-1, keepdims=True)) + lambda_init)


def _subln(o, sw, lambda_init):
    ms = jnp.mean(o * o, axis=-1, keepdims=True)
    return o * lax.rsqrt(ms + EPS) * sw * (1.0 - lambda_init)


def _attn_kernel(lamp_ref, q_ref, k_ref, v_ref, sw_ref, o_ref, m_sc, l_sc, acc_sc, *, tq, lambda_init):
    i = pl.program_id(2)
    q = q_ref[...]
    lane = lax.broadcasted_iota(jnp.int32, q.shape, 1)
    zero = jnp.zeros_like(q)
    qs = (jnp.where(lane < HEAD_DIM_A, q, zero), jnp.where(lane >= HEAD_DIM_A, q, zero))
    m_sc[...] = jnp.full(m_sc.shape, -jnp.inf, F32)
    l_sc[...] = jnp.zeros(l_sc.shape, F32)
    acc_sc[...] = jnp.zeros(acc_sc.shape, F32)

    def step(start, width, masked):
        kb = k_ref[0, :, pl.ds(start, width)]
        vb = v_ref[pl.ds(start, width), :]
        for c in range(2):
            s = _dot(qs[c], kb)
            if masked:
                row = lax.broadcasted_iota(jnp.int32, s.shape, 0)
                col = lax.broadcasted_iota(jnp.int32, s.shape, 1)
                s = jnp.where(col <= row, s, -jnp.inf)
            m_prev = m_sc[c]
            m_new = jnp.maximum(m_prev, jnp.max(s, axis=-1, keepdims=True))
            alpha = jnp.exp(m_prev - m_new)
            ps = [jnp.exp(s[:, g * LANES:(g + 1) * LANES] - m_new) for g in range(width // LANES)]
            l_sc[c] = alpha * l_sc[c] + functools.reduce(lambda a, b: a + b, ps)
            p = jnp.concatenate([x.astype(BF16) for x in ps], axis=1)
            acc_sc[c] = alpha * acc_sc[c] + _dot(p, vb)
            m_sc[c] = m_new

    def quad_body(jq, carry):
        step(pl.multiple_of(jq * 4 * tq, 4 * tq), 4 * tq, False)
        return carry

    lax.fori_loop(0, i // 4, quad_body, 0)

    @pl.when(i % 4 >= 2)
    def _():
        step(pl.multiple_of((i // 4) * 4 * tq, 2 * tq), 2 * tq, False)

    @pl.when(i % 2 == 1)
    def _():
        step(pl.multiple_of((i - 1) * tq, tq), tq, False)

    step(pl.multiple_of(i * tq, tq), tq, True)
    lam = _lam(lamp_ref, lambda_init)
    l0 = jnp.sum(l_sc[0], axis=-1, keepdims=True)
    l1 = jnp.sum(l_sc[1], axis=-1, keepdims=True)
    o = acc_sc[0] / l0 - lam * (acc_sc[1] / l1)
    o_ref[...] = _subln(o, sw_ref[...], lambda_init)


def _attn_prompt(q, k, v, lamp, subln_w, bsz, s_len, tq, lambda_init):
    m = q.shape[0]
    nq = s_len // tq
    return pl.pallas_call(
        functools.partial(_attn_kernel, tq=tq, lambda_init=lambda_init),
        grid=(bsz, N_HEADS_A, nq),
        in_specs=[pl.BlockSpec(lamp.shape, lambda b, h, i: (0, 0)),
                  pl.BlockSpec((tq, LANES), lambda b, h, i: (b * nq + i, h)),
                  pl.BlockSpec((1, LANES, s_len), lambda b, h, i: (b, h, 0)),
                  pl.BlockSpec((s_len, LANES), lambda b, h, i: (b, h)),
                  pl.BlockSpec((1, LANES), lambda b, h, i: (0, 0))],
        out_specs=pl.BlockSpec((tq, LANES), lambda b, h, i: (b * nq + i, h)),
        out_shape=jax.ShapeDtypeStruct((m, A_W), F32),
        scratch_shapes=[pltpu.VMEM((2, tq, LANES), F32)] * 3,
        compiler_params=_cparams(("parallel", "parallel", "parallel")),
        name="attn_prompt",
    )(lamp, q, k, v, subln_w.reshape(1, LANES))


def _decode_stages(qbd, e_ref, k_refs, v_refs, state, res):
    m_prev, l_prev, acc_prev = state
    parts = []
    pair = 2 if len(k_refs) % 2 == 0 else 1
    for p_i in range(0, len(k_refs), pair):
        kt = jnp.concatenate([k_refs[p_i + x][0].astype(BF16) for x in range(pair)], axis=1)
        parts.append(_dot(qbd, kt))
        yield
    s = jnp.concatenate(parts, axis=1)
    m_new = jnp.maximum(m_prev, jnp.max(s, axis=-1, keepdims=True))
    alpha = jnp.exp(m_prev - m_new)
    p = jnp.exp(s - m_new)
    res["m"] = m_new
    res["l"] = alpha * l_prev + jnp.sum(p, axis=-1, keepdims=True)
    nrow = p.shape[0]
    p_rows = jnp.concatenate([p[:, i * PAGE_SIZE:(i + 1) * PAGE_SIZE] for i in range(len(v_refs))], axis=0)
    pe_all = _dot(p_rows.astype(BF16), e_ref[...])
    yield
    pv = jnp.zeros(acc_prev.shape, F32)
    r_i = lax.broadcasted_iota(jnp.int32, (nrow, PAGE_SIZE * N_HEADS_A), 0)
    c_i = lax.broadcasted_iota(jnp.int32, (nrow, PAGE_SIZE * N_HEADS_A), 1)
    own = c_i % N_HEADS_A == r_i // 2
    for p_i, v_ref in enumerate(v_refs):
        pe = jnp.where(own, pe_all[p_i * nrow:(p_i + 1) * nrow], 0.0).astype(BF16)
        v2 = v_ref[0].reshape(PAGE_SIZE * N_HEADS_A, LANES).astype(BF16)
        pv = pv + _dot(pe, v2)
        yield
    res["acc"] = alpha * acc_prev + pv


def _dec_attn_kernel(pt_ref, lamp_ref, q_ref, kn_ref, vn_ref, sw_ref, e_ref, *refs, n_pages_step, lambda_init,
                     delta):
    k_refs = refs[:n_pages_step]
    v_refs = refs[n_pages_step:2 * n_pages_step]
    rest = refs[2 * n_pages_step:]
    if delta is None:
        o_ref, m_sc, l_sc, acc_sc = rest
    else:
        d_in = rest[:N_DELTA_IN]
        gb_ref, zb_ref, onw_ref = d_in[9:]
        o_ref, ob_ref, s_ref, m_sc, l_sc, acc_sc, ext_ref = rest[N_DELTA_IN:]
    j = pl.program_id(1)
    nrow = 2 * N_HEADS_A
    q = q_ref[0]
    hc = lax.broadcasted_iota(jnp.int32, (nrow, A_W), 0)
    lane = lax.broadcasted_iota(jnp.int32, (nrow, A_W), 1)
    qbd = jnp.where(lane // HEAD_DIM_A == hc, jnp.broadcast_to(q, (nrow, A_W)), 0.0).astype(BF16)

    @pl.when(j == 0)
    def _():
        kn = jnp.broadcast_to(kn_ref[0], (nrow, A_W)).astype(BF16).astype(F32)
        m_sc[...] = jnp.sum(qbd.astype(F32) * kn, axis=-1, keepdims=True)
        l_sc[...] = jnp.ones(l_sc.shape, F32)
        vn = vn_ref[0].astype(BF16).astype(F32)
        row_head = lax.broadcasted_iota(jnp.int32, (nrow, LANES), 0) // 2
        acc = jnp.zeros((nrow, LANES), F32)
        for h in range(N_HEADS_A):
            acc = jnp.where(row_head == h, jnp.broadcast_to(vn[:, h * LANES:(h + 1) * LANES], (nrow, LANES)), acc)
        acc_sc[...] = acc

    res = {}
    gens = [_decode_stages(qbd, e_ref, k_refs, v_refs, (m_sc[...], l_sc[...], acc_sc[...]), res)]
    if delta is not None:
        hpu, ng, nc, n_units = delta
        t = pl.program_id(0) * pl.num_programs(1) + j
        unit = jnp.minimum(t, n_units - 1)
        h0 = pl.multiple_of((unit % ng) * hpu, hpu)

        first_chunk = (unit // ng) % nc == 0

        @pl.when(first_chunk & (t < n_units))
        def _():
            s_ref[0, pl.ds(h0, hpu)] = jnp.zeros((hpu, DK_B, DV_B), F32)

        dres = {}

        def conv_then_delta():
            dq, dk, dv = _conv_unit(d_in[0:3], d_in[3:6], d_in[6:9], ext_ref, first_chunk, hpu)
            yield
            yield from _delta_stages(dq, dk, dv, gb_ref[0], s_ref[0, pl.ds(h0, hpu)], hpu, dres)

        gens.append(conv_then_delta())
    _interleave(*gens)
    m_sc[...] = res["m"]
    l_sc[...] = res["l"]
    acc_sc[...] = res["acc"]
    if delta is not None:
        slabs = _delta_finish(dres["o"], zb_ref, onw_ref, hpu)

        @pl.when(t < n_units)
        def _():
            s_ref[0, pl.ds(h0, hpu)] = dres["s"]
            for h, slab in enumerate(slabs):
                ob_ref[:, h * LANES:(h + 1) * LANES] = slab

    @pl.when(j == pl.num_programs(1) - 1)
    def _():
        lam = _lam(lamp_ref, lambda_init)
        o = acc_sc[...] / l_sc[...]
        for h in range(N_HEADS_A):
            oh = o[2 * h:2 * h + 1] - lam * o[2 * h + 1:2 * h + 2]
            o_ref[0, :, h * LANES:(h + 1) * LANES] = _subln(oh, sw_ref[...], lambda_init)


def _delta_units(bsz, s_len, n_steps):
    nc = s_len // CHUNK
    for hpu in (1, 2, 4, 8):
        if bsz * nc * (N_HEADS_B // hpu) <= n_steps:
            return hpu
    return None


def _attn_sample(q, kn, vn, cache_kt, cache_v, page_table, lamp, subln_w, lambda_init, delta_args=None):
    nb, n_pages = page_table.shape
    pps = PAGES_PER_STEP if n_pages % PAGES_PER_STEP == 0 else 1
    nj = n_pages // pps
    row_spec = pl.BlockSpec((1, 1, A_W), lambda b, j, pt: (b, 0, 0))
    d_in_specs, d_out_specs, d_out_shapes, d_args, d_scratch, delta = [], [], [], [], [], None
    if delta_args is not None:
        raw, conv_w, gb, zb, o_norm_w, bsz, s_len, hpu = delta_args
        nc = s_len // CHUNK
        ng = N_HEADS_B // hpu
        n_units = bsz * nc * ng
        delta = (hpu, ng, nc, n_units)
        unit = lambda b, j: jnp.minimum(b * nj + j, n_units - 1)
        d_in_specs, d_args, ob_spec, ext = _delta_operands(
            raw, conv_w, gb, zb, o_norm_w, hpu, lambda b, j, pt: (unit(b, j) // ng, unit(b, j) % ng))
        d_out_specs = [ob_spec, pl.BlockSpec((1, N_HEADS_B, DK_B, DV_B),
                                             lambda b, j, pt: (unit(b, j) // (ng * nc), 0, 0, 0))]
        d_out_shapes = [jax.ShapeDtypeStruct((raw.shape[0], B_W), BF16),
                        jax.ShapeDtypeStruct((bsz, N_HEADS_B, DK_B, DV_B), F32)]
        d_scratch = [ext]

    def page_spec(p_i, shape):
        return pl.BlockSpec((1,) + shape, lambda b, j, pt: (pt[b, j * pps + p_i],) + (0,) * len(shape))

    expand = (jnp.arange(PAGE_SIZE)[:, None] == jnp.arange(PAGE_SIZE * N_HEADS_A)[None, :] // N_HEADS_A).astype(BF16)
    grid_spec = pltpu.PrefetchScalarGridSpec(
        num_scalar_prefetch=1,
        grid=(nb, nj),
        in_specs=[pl.BlockSpec(lamp.shape, lambda b, j, pt: (0, 0)), row_spec, row_spec, row_spec,
                  pl.BlockSpec((1, LANES), lambda b, j, pt: (0, 0)),
                  pl.BlockSpec(expand.shape, lambda b, j, pt: (0, 0))]
                 + [page_spec(p_i, (A_W, PAGE_SIZE)) for p_i in range(pps)]
                 + [page_spec(p_i, (PAGE_SIZE, N_HEADS_A, LANES)) for p_i in range(pps)]
                 + d_in_specs,
        out_specs=[row_spec] + d_out_specs,
        scratch_shapes=[pltpu.VMEM((2 * N_HEADS_A, 1), F32), pltpu.VMEM((2 * N_HEADS_A, 1), F32),
                        pltpu.VMEM((2 * N_HEADS_A, LANES), F32)] + d_scratch,
    )
    outs = pl.pallas_call(
        functools.partial(_dec_attn_kernel, n_pages_step=pps, lambda_init=lambda_init, delta=delta),
        grid_spec=grid_spec,
        out_shape=[jax.ShapeDtypeStruct((nb, 1, A_W), F32)] + d_out_shapes,
        compiler_params=_cparams(("arbitrary", "arbitrary")),
        name="attn_sample",
    )(page_table, lamp, q, kn, vn, subln_w.reshape(1, LANES), expand,
      *([cache_kt] * pps), *([cache_v] * pps), *d_args)
    return outs if delta_args is not None else outs[0]


def _l2n(y):
    return y * lax.rsqrt(jnp.sum(y * y, axis=-1, keepdims=True) + EPS)


def _conv_unit(raw_refs, prev_refs, w_refs, ext_ref, first_chunk, nh):
    hist = SUBLANES - (CONV_W - 1)
    outs = []
    for s in range(3):
        ext_ref[s, 0:SUBLANES, :] = jnp.where(first_chunk, 0.0, prev_refs[s][...])
        ext_ref[s, SUBLANES:SUBLANES + CHUNK, :] = raw_refs[s][...]
        w = w_refs[s][...]
        y = ext_ref[s, hist:hist + CHUNK, :] * w[0:1]
        for jj in range(1, CONV_W):
            y = y + ext_ref[s, hist + jj:hist + jj + CHUNK, :] * w[jj:jj + 1]
        y = _silu(y)
        slabs = [y[:, h * LANES:(h + 1) * LANES] for h in range(nh)]
        if s == 0:
            slabs = [_l2n(x) * (DK_B ** -0.5) for x in slabs]
        elif s == 1:
            slabs = [_l2n(x) for x in slabs]
        outs.append(jnp.stack(slabs))
    return outs


def _bdot(a, b):
    return lax.dot_general(a, b, (((2,), (1,)), ((0,), (0,))), preferred_element_type=F32)


def _bdot_nt(a, b):
    return lax.dot_general(a, b, (((2,), (2,)), ((0,), (0,))), preferred_element_type=F32)


def _bdot_tn(a, b):
    return lax.dot_general(a, b, (((1,), (1,)), ((0,), (0,))), preferred_element_type=F32)


def _split_bdot(a, b):
    ah = a.astype(BF16)
    al = (a - ah.astype(F32)).astype(BF16)
    bh = b.astype(BF16)
    bl = (b - bh.astype(F32)).astype(BF16)
    return _bdot(ah, bh) + _bdot(ah, bl) + _bdot(al, bh)


def _cumsum_rows(x):
    n = x.shape[0]
    row = lax.broadcasted_iota(jnp.int32, x.shape, 0)
    s = 1
    while s < n:
        x = x + jnp.where(row >= s, pltpu.roll(x, s, axis=0), 0.0)
        s *= 2
    return x


def _delta_stages(q, k, v, gb, s_old, nh, res):
    lane_bcast = lambda x, off: jnp.stack(
        [jnp.broadcast_to(x[:, off + h:off + h + 1], (CHUNK, LANES)) for h in range(nh)])
    gc_all = _cumsum_rows(gb)
    gc_t = gc_all.T
    gc = lane_bcast(gc_all, 0)
    gc_row = jnp.stack([gc_t[h:h + 1, :] for h in range(nh)])
    beta = lane_bcast(gb, nh)
    row = lax.broadcasted_iota(jnp.int32, (1, CHUNK, CHUNK), 1)
    col = lax.broadcasted_iota(jnp.int32, (1, CHUNK, CHUNK), 2)
    tril = col <= row
    strict = col < row
    eye = jnp.where(row == col, 1.0, 0.0)
    decay = jnp.exp(jnp.where(tril, gc[:, :, :CHUNK] - gc_row, -jnp.inf))
    kb = k * beta
    k16 = k.astype(BF16)
    lmat = jnp.where(strict, _bdot_nt(kb.astype(BF16), k16) * decay, 0.0)
    qk = jnp.where(tril, _bdot_nt(q.astype(BF16), k16) * decay, 0.0)
    yield
    tinv = eye - lmat
    pw = lmat
    for _ in range(5):
        pw = _split_bdot(pw, pw)
        yield
        tinv = tinv + _split_bdot(tinv, pw)
        yield
    t16 = tinv.astype(BF16)
    u = _bdot(t16, (v * beta).astype(BF16))
    wk = _bdot(t16, (kb * jnp.exp(gc)).astype(BF16))
    s16 = s_old.astype(BF16)
    yield
    v_new = u - _bdot(wk.astype(BF16), s16)
    vn16 = v_new.astype(BF16)
    yield
    res["o"] = _bdot((q * jnp.exp(gc)).astype(BF16), s16) + _bdot(qk.astype(BF16), vn16)
    g_last = gc[:, CHUNK - 1:CHUNK, :]
    k_dec = k * jnp.exp(g_last - gc)
    res["s"] = s_old * jnp.exp(g_last) + _bdot_tn(k_dec.astype(BF16), vn16)


def _interleave(*gens):
    live = list(gens)
    while live:
        for g in list(live):
            try:
                next(g)
            except StopIteration:
                live.remove(g)


def _delta_finish(o, zb_ref, onw_ref, nh):
    on = o * lax.rsqrt(jnp.mean(o * o, axis=-1, keepdims=True) + EPS) * onw_ref[...]
    return [(on[h] * zb_ref[:, h * LANES:(h + 1) * LANES]).astype(BF16) for h in range(nh)]


N_DELTA_IN = 12


def _delta_kernel(*refs):
    d_in = refs[:N_DELTA_IN]
    ob_ref, s_ref, ext_ref = refs[N_DELTA_IN:]
    gb_ref, zb_ref, onw_ref = d_in[9:]
    c = pl.program_id(1)
    nh = N_HEADS_B

    @pl.when(c == 0)
    def _():
        s_ref[...] = jnp.zeros(s_ref.shape, F32)

    q, k, v = _conv_unit(d_in[0:3], d_in[3:6], d_in[6:9], ext_ref, c == 0, nh)
    res = {}
    _interleave(_delta_stages(q, k, v, gb_ref[0], s_ref[0], nh, res))
    s_ref[0] = res["s"]
    for h, slab in enumerate(_delta_finish(res["o"], zb_ref, onw_ref, nh)):
        ob_ref[:, h * LANES:(h + 1) * LANES] = slab


def _delta_operands(raw, conv_w, gb, zb, o_norm_w, hpu, where):
    ng = N_HEADS_B // hpu
    wd = hpu * LANES
    rc = lambda *a: where(*a)[0]
    hg = lambda *a: where(*a)[1]
    cur = lambda s: pl.BlockSpec((CHUNK, wd), lambda *a, s=s: (rc(*a), s * ng + hg(*a)))
    prev = lambda s: pl.BlockSpec(
        (SUBLANES, wd), lambda *a, s=s: (jnp.maximum(rc(*a) * (CHUNK // SUBLANES) - 1, 0), s * ng + hg(*a)))
    taps = lambda s: pl.BlockSpec((CONV_W, wd), lambda *a, s=s: (0, s * ng + hg(*a)))
    specs = ([cur(s) for s in range(3)] + [prev(s) for s in range(3)] + [taps(s) for s in range(3)]
             + [pl.BlockSpec((1, CHUNK, LANES), lambda *a: (hg(*a), rc(*a), 0)),
                pl.BlockSpec((CHUNK, wd), lambda *a: (rc(*a), hg(*a))),
                pl.BlockSpec((1, LANES), lambda *a: (0, 0))])
    args = [raw] * 6 + [conv_w] * 3 + [_group_gates(gb, hpu), zb, o_norm_w.reshape(1, LANES)]
    out_spec = pl.BlockSpec((CHUNK, wd), lambda *a: (rc(*a), hg(*a)))
    scratch = pltpu.VMEM((3, CHUNK + SUBLANES, wd), F32)
    return specs, args, out_spec, scratch


def _group_gates(gb, hpu):
    m = gb.shape[0]
    ng = N_HEADS_B // hpu
    g = gb[:, :N_HEADS_B].reshape(m, ng, hpu)
    beta = gb[:, N_HEADS_B:2 * N_HEADS_B].reshape(m, ng, hpu)
    gbg = jnp.transpose(jnp.concatenate([g, beta], axis=-1), (1, 0, 2))
    return jnp.pad(gbg, ((0, 0), (0, 0), (0, LANES - 2 * hpu)))


def _delta_prompt(raw, conv_w, gb, zb, o_norm_w, bsz, s_len):
    m = raw.shape[0]
    nc = s_len // CHUNK
    specs, args, out_spec, scratch = _delta_operands(raw, conv_w, gb, zb, o_norm_w, N_HEADS_B,
                                                     lambda b, c: (b * nc + c, 0))
    return pl.pallas_call(
        _delta_kernel,
        grid=(bsz, nc),
        in_specs=specs,
        out_specs=[out_spec, pl.BlockSpec((1, N_HEADS_B, DK_B, DV_B), lambda b, c: (b, 0, 0, 0))],
        out_shape=[jax.ShapeDtypeStruct((m, B_W), BF16),
                   jax.ShapeDtypeStruct((bsz, N_HEADS_B, DK_B, DV_B), F32)],
        scratch_shapes=[scratch],
        compiler_params=_cparams(("parallel", "arbitrary")),
        name="delta_prompt",
    )(*args)


def _dec_delta_kernel(u_ref, buf_ref, w_ref, gb_ref, zb_ref, onw_ref, s0_ref, ob_ref, s_ref):
    w = w_ref[...]
    buf = buf_ref[0]
    y = buf[0:1] * w[0:1] + buf[1:2] * w[1:2] + buf[2:3] * w[2:3] + u_ref[0] * w[3:4]
    y = _silu(y)
    gb = gb_ref[0]
    r = lax.broadcasted_iota(jnp.int32, (DK_B, LANES), 0)
    cidx = lax.broadcasted_iota(jnp.int32, (DK_B, LANES), 1)
    eye = r == cidx

    def column(x):
        return jnp.sum(jnp.where(eye, jnp.broadcast_to(x, (DK_B, LANES)), 0.0), axis=-1, keepdims=True)

    for h in range(N_HEADS_B):
        sl = slice(h * LANES, (h + 1) * LANES)
        q = _l2n(y[:, h * LANES:(h + 1) * LANES]) * (DK_B ** -0.5)
        k = _l2n(y[:, B_W + h * LANES:B_W + (h + 1) * LANES])
        v = y[:, 2 * B_W + h * LANES:2 * B_W + (h + 1) * LANES]
        a = jnp.exp(gb[:, h:h + 1])
        beta = gb[:, N_HEADS_B + h:N_HEADS_B + h + 1]
        s_old = s0_ref[0, h]
        k_col = column(k)
        q_col = column(q)
        v_new = beta * (v - a * jnp.sum(k_col * s_old, axis=0, keepdims=True))
        s_new = a * s_old + k_col * v_new
        s_ref[0, h] = s_new
        o = jnp.sum(q_col * s_new, axis=0, keepdims=True)
        on = o * lax.rsqrt(jnp.mean(o * o, axis=-1, keepdims=True) + EPS) * onw_ref[...]
        ob_ref[0, :, sl] = (on * zb_ref[0, :, sl]).astype(BF16)


def _delta_sample(u, conv_buf, conv_w, gb, zb, o_norm_w, s0):
    nb, _, cdim = u.shape
    row = lambda wd: pl.BlockSpec((1, 1, wd), lambda b: (b, 0, 0))
    st = pl.BlockSpec((1, N_HEADS_B, DK_B, DV_B), lambda b: (b, 0, 0, 0))
    return pl.pallas_call(
        _dec_delta_kernel,
        grid=(nb,),
        in_specs=[row(cdim), pl.BlockSpec((1, CONV_W - 1, cdim), lambda b: (b, 0, 0)),
                  pl.BlockSpec((CONV_W, cdim), lambda b: (0, 0)), row(LANES), row(B_W),
                  pl.BlockSpec((1, LANES), lambda b: (0, 0)), st],
        out_specs=[row(B_W), st],
        out_shape=[jax.ShapeDtypeStruct((nb, 1, B_W), BF16),
                   jax.ShapeDtypeStruct(s0.shape, F32)],
        compiler_params=_cparams(("parallel",)),
        name="delta_sample",
    )(u, conv_buf, conv_w, gb, zb, o_norm_w.reshape(1, LANES), s0)


def _out_kernel(oa_ref, za_ref, ob_ref, ga_ref, gbt_ref, x_ref, gate_ref, wa_ref, wb_ref, wo_ref, y_ref):
    ya = _dot((oa_ref[...] * za_ref[...]).astype(BF16), wa_ref[...])
    yb = _dot(ob_ref[...], wb_ref[...])
    mrg = ga_ref[...] * ya + gbt_ref[...] * yb
    y_ref[0] = x_ref[0] + gate_ref[0] * _dot(mrg.astype(BF16), wo_ref[...])


def _out_proj(oa, za, ob, gates, x, gate, wa, wb, wo, tm):
    b, s, d = x.shape
    nt = s // tm
    ts = tm if gate.shape[1] == s else 1
    gate_map = (lambda i, j: (i, j, 0)) if ts == tm else (lambda i, j: (i, 0, 0))
    rows = lambda wd, cb=0: pl.BlockSpec((tm, wd), lambda i, j, cb=cb: (i * nt + j, cb))
    const = lambda shp: pl.BlockSpec(shp, lambda i, j: (0, 0), pipeline_mode=pl.Buffered(1))
    return pl.pallas_call(
        _out_kernel,
        grid=(b, nt),
        in_specs=[rows(A_W), rows(A_W), rows(B_W), rows(d, 0), rows(d, 1),
                  pl.BlockSpec((1, tm, d), lambda i, j: (i, j, 0)),
                  pl.BlockSpec((1, ts, d), gate_map),
                  const(wa.shape), const(wb.shape), const(wo.shape)],
        out_specs=pl.BlockSpec((1, tm, d), lambda i, j: (i, j, 0)),
        out_shape=jax.ShapeDtypeStruct((b, s, d), F32),
        compiler_params=_cparams(("parallel", "parallel")),
        name="out_proj",
    )(oa, za, ob, gates, gates, x, gate, wa, wb, wo)


def _tile(n, pref):
    t = min(n, pref)
    while n % t:
        t //= 2
    return t


def _project(h2, wts, tm, kt_dims=None):
    (w_main, cdim, gmat, qnw, knw, alog, dtb) = wts
    tn = A_W if h2.shape[0] > tm else SMALL_M_TN
    seg = lambda col, n: (w_main, col // tn, n)
    (q16,) = _proj("q", h2, seg(0, A_W), [gmat, qnw], [BF16], tm, tn)
    if kt_dims is None:
        k32, k16 = _proj("k", h2, seg(A_W, A_W), [gmat, knw], [F32, BF16], tm, tn)
    else:
        knw_col = knw[0, :HEAD_DIM_A].reshape(1, HEAD_DIM_A, 1)
        k32, k16 = _proj_kt(h2, (w_main, 1), knw_col, kt_dims[0], kt_dims[1], tm)
    v32, v16 = _proj("v", h2, seg(2 * A_W, A_W), [], [F32, BF16], tm, tn)
    (za,) = _proj("silu", h2, seg(3 * A_W, A_W), [], [F32], tm, tn)
    (raw,) = _proj("raw", h2, seg(4 * A_W, cdim), [], [F32], tm, tn)
    (zb,) = _proj("silu", h2, seg(4 * A_W + cdim, B_W), [], [F32], tm, tn)
    o_ab = 4 * A_W + cdim + B_W
    (gb,) = _proj("ab", h2, (w_main, o_ab // LANES, LANES), [alog, dtb], [F32], tm, LANES)
    n_gates = w_main.shape[0] - o_ab - 2 * N_HEADS_B
    (gates,) = _proj("sigmoid", h2, (w_main, o_ab // tn, n_gates, 2 * N_HEADS_B), [], [F32], tm, tn)
    return q16, k32, k16, v32, v16, za, raw, zb, gb, gates


def kernel(x_prompt, x_sample, c_prompt, c_sample, cache_k, cache_v, page_table, state_ssm, state_conv, w_ada, b_ada, norm_w, w_in, q_norm_w, k_norm_w, lam_q1, lam_k1, lam_q2, lam_k2, subln_w, conv_w, a_log, dt_bias, o_norm_w, w_branch_a, w_branch_b, w_out):
    depth = w_ada.shape[0]
    bsz, s_len, d = x_prompt.shape
    nb, t_dec, _ = x_sample.shape
    assert t_dec == 1 and s_len % CHUNK == 0 and s_len >= CONV_W - 1
    m_p = bsz * s_len
    cdim = conv_w.shape[-1]
    o_ab = 4 * A_W + cdim + B_W
    assert cdim % A_W == 0 and w_in.shape[-1] == o_ab + 2 * N_HEADS_B + 2 * d and (2 * d) % A_W == 0

    gi = jnp.arange(LANES) // HEAD_DIM_A
    gmat = jnp.where(gi[:, None] == gi[None, :], 1.0 / HEAD_DIM_A, 0.0).astype(BF16)

    yp, ys = x_prompt, x_sample
    outs = [[] for _ in range(8)]
    for li in range(depth):
        lambda_init = 0.8 - 0.6 * math.exp(-0.3 * li)
        w_main = w_in[li].T
        pad8 = lambda vec: jnp.pad(vec, (0, LANES - N_HEADS_B)).reshape(1, LANES)
        wts = (w_main, cdim, gmat,
               jnp.tile(q_norm_w[li], 2 * N_HEADS_A).reshape(1, A_W),
               jnp.tile(k_norm_w[li], 2 * N_HEADS_A).reshape(1, A_W),
               pad8(a_log[li]), pad8(dt_bias[li]))
        wa16 = w_branch_a[li].astype(BF16)
        wbb16 = w_branch_b[li].astype(BF16)
        wo16 = w_out[li].astype(BF16)
        lamp = jnp.stack([lam_q1[li], lam_k1[li], lam_q2[li], lam_k2[li]])

        n_c = bsz + nb
        c_all = jnp.pad(jnp.concatenate([c_prompt, c_sample], axis=0), ((0, (-n_c) % 16), (0, 0)))
        mod = _ada(c_all, w_ada[li], b_ada[li])
        shift, scale, gate = mod[:, :d], mod[:, d:2 * d], mod[:, 2 * d:]

        tm = _tile(s_len, 1024)
        h = _hnorm(yp, shift[:bsz, None], scale[:bsz, None], norm_w[li], tm)
        q16, kt32, kt16, v32, v16, za, raw, zb, gb, gates = _project(h.reshape(m_p, d), wts, tm, (bsz, s_len))
        xs = ys.reshape(1, nb, d)
        sl_s = slice(bsz, bsz + nb)
        h_s = _hnorm(xs, shift[None, sl_s], scale[None, sl_s], norm_w[li], nb)
        q16_s, k32_s, _, v32_s, _, za_s, raw_s, zb_s, gb_s, gates_s = _project(h_s.reshape(nb, d), wts, nb)

        oa = _attn_prompt(q16, kt16, v16, lamp, subln_w[li], bsz, s_len, _tile(s_len, 512), lambda_init)
        n_pool = cache_k.shape[1]
        cache_kt = jnp.transpose(cache_k[li], (0, 2, 3, 4, 1)).reshape(n_pool, A_W, PAGE_SIZE)
        dec_args = (q16_s.astype(F32).reshape(nb, 1, A_W), k32_s.reshape(nb, 1, A_W), v32_s.reshape(nb, 1, A_W),
                    cache_kt, cache_v[li], page_table, lamp, subln_w[li], lambda_init)
        n_pages = page_table.shape[1]
        n_steps = nb * (n_pages // (PAGES_PER_STEP if n_pages % PAGES_PER_STEP == 0 else 1))
        hpu = _delta_units(bsz, s_len, n_steps)
        if hpu is None:
            oa_s = _attn_sample(*dec_args)
            ob, ssm_p = _delta_prompt(raw, conv_w[li], gb, zb, o_norm_w[li], bsz, s_len)
        else:
            oa_s, ob, ssm_p = _attn_sample(
                *dec_args, delta_args=(raw, conv_w[li], gb, zb, o_norm_w[li], bsz, s_len, hpu))
        ob_s, ssm_s = _delta_sample(raw_s.reshape(nb, 1, cdim), state_conv[li], conv_w[li],
                                    gb_s.reshape(nb, 1, LANES), zb_s.reshape(nb, 1, B_W), o_norm_w[li],
                                    state_ssm[li])

        yp = _out_proj(oa, za, ob, gates, yp, gate[:bsz, None], wa16, wbb16, wo16, _tile(s_len, 256))
        ys = _out_proj(oa_s.reshape(nb, A_W), za_s, ob_s.reshape(nb, B_W), gates_s, xs, gate[None, sl_s],
                       wa16, wbb16, wo16, nb).reshape(nb, 1, d)
        outs[0].append(jnp.transpose(kt32.reshape(bsz, N_HEADS_A, 2, HEAD_DIM_A, s_len), (0, 4, 1, 2, 3)))
        outs[1].append(v32.reshape(bsz, s_len, N_HEADS_A, 2 * HEAD_DIM_A))
        outs[2].append(k32_s.reshape(nb, 1, N_HEADS_A, 2, HEAD_DIM_A))
        outs[3].append(v32_s.reshape(nb, 1, N_HEADS_A, 2 * HEAD_DIM_A))
        outs[4].append(ssm_p)
        outs[5].append(ssm_s)
        outs[6].append(raw.reshape(bsz, s_len, cdim)[:, s_len - (CONV_W - 1):])
        outs[7].append(jnp.concatenate([state_conv[li][:, 1:], raw_s.reshape(nb, 1, cdim)], axis=1))

    st = [jnp.stack(o) for o in outs]
    return (yp, ys, st[0], st[1], st[2], st[3], st[4], st[5], st[6], st[7])
```
